```python
import math
import jax
import jax.numpy as jnp
from jax import lax
import numpy as np


D_MODEL = 2048
BATCH = 1
SEQ = 8192
DEPTH = 4

GRID_W = 64
CTX_LEN = 256
MIX_W = D_MODEL
N_GROUPS = 4
GROUP_W = MIX_W // N_GROUPS
NORM_EPS = 1e-6
S5_CH = 16
S5_GROUPS = GROUP_W // S5_CH
S5_STATE = 64
RET_HEADS = 4
RET_DK = GROUP_W // RET_HEADS
RET_CHUNK = 128
ROPE_BASE = 10000.0
LRU_BLOCKS = 8
LRU_BW = GROUP_W // LRU_BLOCKS
LRU_CONV = 4
LRU_C = 8.0
RWKV_HEAD = 64
RWKV_HEADS = GROUP_W // RWKV_HEAD
RWKV_DECAY_RANK = max(32, int(round(1.8 * math.sqrt(D_MODEL) / 32)) * 32)
RWKV_A_RANK = RWKV_DECAY_RANK
RWKV_GATE_RANK = max(32, int(round(0.6 * D_MODEL ** 0.8 / 32)) * 32)
RWKV_NCH = 3 * GROUP_W + RWKV_DECAY_RANK + RWKV_A_RANK + RWKV_GATE_RANK
RWKV_LN_EPS = 64e-5
N_IN = 7 * GROUP_W + RWKV_NCH
N_EXPERTS = 32
TOP_K = 4
D_FF = 7 * D_MODEL // 16
SWIGLU_ALPHA = 1.702
SWIGLU_LIMIT = 7.0
MOE_BLOCK = 256
F32 = jnp.float32

kernel_name = 'hybrid_parallel_heads_diffusion_block'


def _rms(x):
    xf = x.astype(F32)
    return xf * lax.rsqrt(jnp.mean(xf * xf, axis=-1, keepdims=True) + NORM_EPS)


def rmsnorm(x, g):
    return (_rms(x) * g.astype(F32)).astype(x.dtype)


def _flip(t):
    return jnp.flip(t, axis=1)


def _rope1d(x, pos):
    m = x.shape[-1] // 2
    inv = ROPE_BASE ** (-jnp.arange(m, dtype=F32) / m)
    ang = pos.astype(F32)[:, None] * inv[None, :]
    cos, sin = jnp.cos(ang)[:, None, :], jnp.sin(ang)[:, None, :]
    x1, x2 = x[..., :m], x[..., m:]
    return jnp.concatenate([x1 * cos - x2 * sin, x1 * sin + x2 * cos], axis=-1)


def rope2d(x):
    t = jnp.arange(x.shape[1])
    half = x.shape[-1] // 2
    return jnp.concatenate([_rope1d(x[..., :half], t // GRID_W),
                            _rope1d(x[..., half:], t % GRID_W)], axis=-1)


def _complex_scan(a_re, a_im, b_re, b_im):
    def combine(e1, e2):
        a1r, a1i, b1r, b1i = e1
        a2r, a2i, b2r, b2i = e2
        return (a2r * a1r - a2i * a1i, a2r * a1i + a2i * a1r,
                a2r * b1r - a2i * b1i + b2r, a2r * b1i + a2i * b1r + b2i)
    _, _, h_re, h_im = lax.associative_scan(combine, (a_re, a_im, b_re, b_im), axis=1)
    return h_re, h_im


def s5_scan(u, lam_re, lam_im, log_dt, b_re, b_im, c_re, c_im, h0_re, h0_im):
    dt = jnp.exp(log_dt.astype(F32))[:, None]
    lr = jnp.minimum(lam_re.astype(F32), -1e-4)
    li = lam_im.astype(F32)
    mag = jnp.exp(lr * dt)
    ab_re, ab_im = mag * jnp.cos(li * dt), mag * jnp.sin(li * dt)
    den = lr * lr + li * li
    f_re = ((ab_re - 1.0) * lr + ab_im * li) / den
    f_im = (ab_im * lr - (ab_re - 1.0) * li) / den
    bb_re = f_re[..., None] * b_re - f_im[..., None] * b_im
    bb_im = f_re[..., None] * b_im + f_im[..., None] * b_re
    bu_re = jnp.einsum('blgc,gnc->blgn', u, bb_re)
    bu_im = jnp.einsum('blgc,gnc->blgn', u, bb_im)
    bu_re = bu_re.at[:, 0].add(ab_re * h0_re - ab_im * h0_im)
    bu_im = bu_im.at[:, 0].add(ab_re * h0_im + ab_im * h0_re)
    h_re, h_im = _complex_scan(jnp.broadcast_to(ab_re, bu_re.shape),
                               jnp.broadcast_to(ab_im, bu_im.shape), bu_re, bu_im)
    y = jnp.einsum('blgn,gcn->blgc', h_re, c_re) - jnp.einsum('blgn,gcn->blgc', h_im, c_im)
    return y, h_re[:, -1], h_im[:, -1]


def s5_mixer(uc, ux, lam_re, lam_im, log_dt, b_re, b_im, c_re, c_im, d_skip, w_glu, b_glu):
    def groups(u):
        return u.astype(F32).reshape(u.shape[:2] + (S5_GROUPS, S5_CH))

    def run(u, d, h0_re, h0_im):
        if d == 1:
            u = _flip(u)
        y, hr, hi = s5_scan(u, lam_re[d], lam_im[d], log_dt[d], b_re[d], b_im[d],
                            c_re[d], c_im[d], h0_re, h0_im)
        return (_flip(y) if d == 1 else y), hr, hi

    def finish(y, u):
        y = (y + d_skip.reshape(S5_GROUPS, S5_CH) * u).reshape(u.shape[:2] + (GROUP_W,))
        y = jax.nn.gelu(y)
        return y * jax.nn.sigmoid(y @ w_glu + b_glu)

    gc, gx = groups(uc), groups(ux)
    zero = jnp.zeros((ux.shape[0], S5_GROUPS, S5_STATE), F32)
    ycf, hf_re, hf_im = run(gc, 0, zero, zero)
    ycb, hb_re, hb_im = run(gc, 1, zero, zero)
    yxf, _, _ = run(gx, 0, hf_re, hf_im)
    yxb, _, _ = run(gx, 1, hb_re, hb_im)
    return finish(ycf + ycb, gc), finish(yxf + yxb, gx)


def retention_chunks(q, k, v, log_g, s0, inclusive):
    B, L, H, _ = q.shape
    dv = v.shape[-1]
    C = RET_CHUNK
    n = L // C
    idx = jnp.arange(C, dtype=F32)
    rel = idx[:, None] - idx[None, :]
    mask = (rel >= 0) if inclusive else (rel > 0)
    d_intra = jnp.where(mask[None], jnp.exp(jnp.maximum(rel, 0.0)[None] * log_g[:, None, None]), 0.0)
    d_q = jnp.exp((idx[None, :] + 1.0) * log_g[:, None]).T
    d_k = jnp.exp((C - 1.0 - idx[None, :]) * log_g[:, None]).T
    d_c = jnp.exp(C * log_g)

    def chunks(t):
        return t.reshape(B, n, C, H, t.shape[-1]).transpose(1, 0, 2, 3, 4)

    def step(S, inp):
        qc, kc, vc = inp
        sc = jnp.einsum('bihd,bjhd->bhij', qc, kc) * d_intra[None]
        o = (jnp.einsum('bhij,bjhe->bihe', sc, vc)
             + jnp.einsum('bihd,bhde->bihe', qc, S) * d_q[None, :, :, None])
        S = S * d_c[None, :, None, None] + jnp.einsum('bjhd,bjhe->bhde', kc * d_k[None, :, :, None], vc)
        return S, o

    S, o = lax.scan(step, s0, (chunks(q), chunks(k), chunks(v)))
    return o.transpose(1, 0, 2, 3, 4).reshape(B, L, H, dv), S


def retention_mixer(pc, px, decay):
    log_g = -jnp.exp(decay.astype(F32))

    def heads(p, rotate):
        q, k, v, g = jnp.split(p.astype(F32), 4, axis=-1)
        shp = p.shape[:2] + (RET_HEADS, RET_DK)
        q, k, v = q.reshape(shp), k.reshape(shp) * RET_DK ** -0.5, v.reshape(shp)
        if rotate:
            q, k = rope2d(q), rope2d(k)
        return q, k, v, g

    def bidir(q, k, v, s_f, s_b):
        o_f, s_f = retention_chunks(q, k, v, log_g[0], s_f, True)
        o_b, s_b = retention_chunks(_flip(q), _flip(k), _flip(v), log_g[1], s_b, False)
        return o_f + _flip(o_b), s_f, s_b

    def finish(o, g):
        return _rms(o).reshape(g.shape) * jax.nn.silu(g)

    zero = jnp.zeros((px.shape[0], RET_HEADS, RET_DK, RET_DK), F32)
    qc, kc, vc, gc = heads(pc, False)
    oc, s_f, s_b = bidir(qc, kc, vc, zero, zero)
    qx, kx, vx, gx = heads(px, True)
    ox, _, _ = bidir(qx, kx, vx, s_f, s_b)
    return finish(oc, gc), finish(ox, gx)


def _real_scan(a, b, h0):
    b = b.at[:, 0].add(a[:, 0] * h0)

    def combine(e1, e2):
        a1, b1 = e1
        a2, b2 = e2
        return a1 * a2, a2 * b1 + b2
    return lax.associative_scan(combine, (a, b), axis=1)[1]


def rglru_mixer(pc, px, conv_w, conv_b, lam, w_r, b_r, w_i, b_i):
    def conv_branch(p):
        xin, gate = jnp.split(p.astype(F32), 2, axis=-1)
        L = xin.shape[1]
        left = LRU_CONV // 2
        xp = jnp.pad(xin, ((0, 0), (left, LRU_CONV - 1 - left), (0, 0)))
        xc = sum(xp[:, j:j + L] * conv_w[j] for j in range(LRU_CONV)) + conv_b
        return xc, gate

    def block_diag(xc, w):
        xb = xc.reshape(xc.shape[:2] + (LRU_BLOCKS, LRU_BW))
        return jnp.einsum('blhi,hij->blhj', xb, w).reshape(xc.shape)

    def run(xc, d, h0):
        if d == 1:
            xc = _flip(xc)
        r = jax.nn.sigmoid(block_diag(xc, w_r[d]) + b_r[d])
        i = jax.nn.sigmoid(block_diag(xc, w_i[d]) + b_i[d])
        log_a = -LRU_C * r * jax.nn.softplus(-lam[d].astype(F32))
        h = _real_scan(jnp.exp(log_a), jnp.sqrt(-jnp.expm1(2.0 * log_a)) * (i * xc), h0)
        return (_flip(h) if d == 1 else h), h[:, -1]

    xcc, gcc = conv_branch(pc)
    xcx, gcx = conv_branch(px)
    zero = jnp.zeros((px.shape[0], GROUP_W), F32)
    hcf, s_f = run(xcc, 0, zero)
    hcb, s_b = run(xcc, 1, zero)
    hxf, _ = run(xcx, 0, s_f)
    hxb, _ = run(xcx, 1, s_b)
    return (hcf + hcb) * jax.nn.gelu(gcc), (hxf + hxb) * jax.nn.gelu(gcx)


def _qshift(z):
    B, L, Cn = z.shape
    rows = L // GRID_W
    zg = jnp.pad(z.reshape(B, rows, GRID_W, Cn // 4, 4), ((0, 0), (1, 1), (1, 1), (0, 0), (0, 0)))
    parts = [zg[:, :-2, 1:-1, :, 0], zg[:, 2:, 1:-1, :, 1],
             zg[:, 1:-1, :-2, :, 2], zg[:, 1:-1, 2:, :, 3]]
    return jnp.stack(parts, axis=-1).reshape(B, L, Cn)


def _seq_shift(z):
    B, L, Cn = z.shape
    zz = z.reshape(B, L, Cn // 2, 2)
    prev = jnp.pad(zz[..., 0], ((0, 0), (1, 0), (0, 0)))[:, :-1]
    nxt = jnp.pad(zz[..., 1], ((0, 0), (0, 1), (0, 0)))[:, 1:]
    return jnp.stack([prev, nxt], axis=-1).reshape(B, L, Cn)


def _rwkv7_scan(r, w, k, v, kk, a, s0, inclusive):
    xs = tuple(jnp.moveaxis(t, 1, 0) for t in (r, w, k, v, kk, a))

    def step(S, inp):
        r_t, w_t, k_t, v_t, kk_t, a_t = inp
        sa = jnp.einsum('bhij,bhj->bhi', S, -kk_t)
        S_new = (S * w_t[:, :, None, :] + sa[..., :, None] * (kk_t * a_t)[:, :, None, :]
                 + v_t[..., :, None] * k_t[:, :, None, :])
        y = jnp.einsum('bhij,bhj->bhi', S_new if inclusive else S, r_t)
        return S_new, y

    S, ys = lax.scan(step, s0, xs)
    return jnp.moveaxis(ys, 0, 1), S


def rwkv7_mixer(pc, px, mu, w0, w_up, a0, a_up, g_up, k_k, k_a, r_k, ln_w, ln_b):
    split_at = np.cumsum([GROUP_W] * 3 + [RWKV_DECAY_RANK, RWKV_A_RANK]).tolist()

    def heads(t):
        return t.reshape(t.shape[:2] + (RWKV_HEADS, RWKV_HEAD))

    def prep(p, shift):
        p = p.astype(F32)
        z = p + (shift(p) - p) * mu
        r, k, v, wc, ac, gc = jnp.split(z, split_at, axis=-1)
        g = jax.nn.sigmoid(gc) @ g_up
        kk = heads(k * k_k)
        kk = kk * lax.rsqrt(jnp.maximum(jnp.sum(kk * kk, axis=-1, keepdims=True), 1e-12))
        per_dir = []
        for d in range(2):
            w_log = -jax.nn.softplus(-(w0[d] + jnp.tanh(wc) @ w_up[d])) - 0.5
            a = jax.nn.sigmoid(a0[d] + ac @ a_up[d])
            per_dir.append((heads(jnp.exp(-jnp.exp(w_log))),
                            heads(k * (1.0 + (a - 1.0) * k_a)), heads(a)))
        return heads(r), heads(k), heads(v), kk, g, per_dir

    def run(r, v, kk, per_dir, d, s0):
        w, kd, a = per_dir[d]
        if d == 0:
            return _rwkv7_scan(r, w, kd, v, kk, a, s0, True)
        y, s = _rwkv7_scan(_flip(r), _flip(w), _flip(kd), _flip(v), _flip(kk), _flip(a), s0, False)
        return _flip(y), s

    def finish(y, r, k, v, g):
        mean = jnp.mean(y, axis=-1, keepdims=True)
        var = jnp.mean(jnp.square(y - mean), axis=-1, keepdims=True)
        yn = ((y - mean) * lax.rsqrt(var + RWKV_LN_EPS)).reshape(g.shape) * ln_w + ln_b
        bonus = (jnp.sum(r * k * r_k, axis=-1, keepdims=True) * v).reshape(g.shape)
        return (yn + bonus) * g

    rc, kc, vc, kkc, gc, dc = prep(pc, _seq_shift)
    rx, kx, vx, kkx, gx, dx = prep(px, _qshift)
    zero = jnp.zeros((px.shape[0], RWKV_HEADS, RWKV_HEAD, RWKV_HEAD), F32)
    ycf, s_f = run(rc, vc, kkc, dc, 0, zero)
    ycb, s_b = run(rc, vc, kkc, dc, 1, zero)
    yxf, _ = run(rx, vx, kkx, dx, 0, s_f)
    yxb, _ = run(rx, vx, kkx, dx, 1, s_b)
    return finish(ycf + ycb, rc, kc, vc, gc), finish(yxf + yxb, rx, kx, vx, gx)


def merge_heads(ys, gain, w_out, dtype):
    y = _rms(jnp.stack(ys, axis=-2)) * gain.reshape(N_GROUPS, GROUP_W).astype(F32)
    return (y.reshape(y.shape[:2] + (MIX_W,)) @ w_out).astype(dtype)


def moe_ffn(h, router_w, router_b, w1, b1, w2, b2):
    T, D = h.shape
    logits = h.astype(F32) @ router_w.astype(F32) + router_b.astype(F32)
    top_v, top_e = lax.top_k(logits, TOP_K)
    gates = jax.nn.softmax(top_v, axis=-1)
    n = T * TOP_K
    flat_e = top_e.reshape(n)
    flat_t = jnp.repeat(jnp.arange(T, dtype=jnp.int32), TOP_K)
    order = jnp.argsort(flat_e)
    se, st, sg = flat_e[order], flat_t[order], gates.reshape(n)[order]
    counts = jnp.bincount(flat_e, length=N_EXPERTS)
    padded = (counts + MOE_BLOCK - 1) // MOE_BLOCK * MOE_BLOCK
    starts = jnp.cumsum(counts) - counts
    pends = jnp.cumsum(padded)
    dest = (pends - padded)[se] + jnp.arange(n, dtype=jnp.int32) - starts[se]
    n_blocks = -(-n // MOE_BLOCK) + N_EXPERTS
    rows_tok = jnp.zeros((n_blocks * MOE_BLOCK,), jnp.int32).at[dest].set(st)
    rows_gate = jnp.zeros((n_blocks * MOE_BLOCK,), F32).at[dest].set(sg)
    block_e = jnp.minimum(jnp.searchsorted(pends, jnp.arange(n_blocks, dtype=jnp.int32) * MOE_BLOCK,
                                           side='right'), N_EXPERTS - 1)

    def expert_block(args):
        e, tok = args
        hid = h[tok] @ w1[e] + b1[e]
        x_glu = jnp.minimum(hid[:, :D_FF], SWIGLU_LIMIT)
        x_lin = jnp.clip(hid[:, D_FF:], -SWIGLU_LIMIT, SWIGLU_LIMIT)
        act = x_glu * jax.nn.sigmoid(SWIGLU_ALPHA * x_glu) * (x_lin + 1.0)
        return act @ w2[e] + b2[e]

    y = lax.map(expert_block, (block_e, rows_tok.reshape(n_blocks, MOE_BLOCK)))
    y = y.reshape(-1, D) * rows_gate[:, None]
    return jnp.zeros_like(h).at[rows_tok].add(y.astype(h.dtype))


def setup_inputs(seed: int = 0) -> dict:
    key = jax.random.key(seed)
    ks = iter(jax.random.split(key, 64))

    def nrm(shape, scale):
        return jax.random.normal(next(ks), shape, F32) * scale

    def unif(shape, lo, hi):
        return jax.random.uniform(next(ks), shape, F32, lo, hi)

    L2 = (DEPTH, 2)
    G, N, C, GW = S5_GROUPS, S5_STATE, S5_CH, GROUP_W
    ret_base = jnp.log(-jnp.log1p(-(2.0 ** (-5.0 - jnp.arange(RET_HEADS, dtype=F32)))))
    return {
        'x': nrm((BATCH, SEQ, D_MODEL), 1.0),
        'c': nrm((BATCH, D_MODEL), 1.0),
        'ctx': nrm((BATCH, CTX_LEN, D_MODEL), 1.0),
        'c_ctx': nrm((D_MODEL,), 1.0),
        'w_ada': nrm((DEPTH, D_MODEL, 6 * D_MODEL), 0.5 * D_MODEL ** -0.5),
        'b_ada': nrm((DEPTH, 6 * D_MODEL), 0.01),
        'norm1': 1.0 + nrm((DEPTH, D_MODEL), 0.02),
        'norm2': 1.0 + nrm((DEPTH, D_MODEL), 0.02),
        'w_in': nrm((DEPTH, D_MODEL, N_IN), D_MODEL ** -0.5),
        'w_out': nrm((DEPTH, MIX_W, D_MODEL), MIX_W ** -0.5),
        'mix_gain': 1.0 + nrm((DEPTH, MIX_W), 0.02),
        's5_lam_re': -0.5 + nrm(L2 + (G, N), 0.01),
        's5_lam_im': jnp.pi * jnp.arange(N, dtype=F32) + nrm(L2 + (G, N), 0.01),
        's5_log_dt': unif(L2 + (G,), math.log(1e-3), math.log(1e-1)),
        's5_b_re': nrm(L2 + (G, N, C), (2 * C) ** -0.5),
        's5_b_im': nrm(L2 + (G, N, C), (2 * C) ** -0.5),
        's5_c_re': nrm(L2 + (G, C, N), (2 * N) ** -0.5),
        's5_c_im': nrm(L2 + (G, C, N), (2 * N) ** -0.5),
        's5_d': nrm((DEPTH, GW), 0.5),
        's5_w_glu': nrm((DEPTH, GW, GW), GW ** -0.5),
        's5_b_glu': nrm((DEPTH, GW), 0.01),
        'ret_decay': ret_base + nrm(L2 + (RET_HEADS,), 0.05),
        'lru_conv_w': nrm((DEPTH, LRU_CONV, GW), LRU_CONV ** -0.5),
        'lru_conv_b': nrm((DEPTH, GW), 0.01),
        'lru_lam': (lambda p: jnp.log(p) - jnp.log1p(-p))(unif(L2 + (GW,), 0.9, 0.999) ** (1.0 / LRU_C)),
        'lru_w_r': nrm(L2 + (LRU_BLOCKS, LRU_BW, LRU_BW), LRU_BW ** -0.5),
        'lru_b_r': nrm(L2 + (GW,), 0.01),
        'lru_w_i': nrm(L2 + (LRU_BLOCKS, LRU_BW, LRU_BW), LRU_BW ** -0.5),
        'lru_b_i': nrm(L2 + (GW,), 0.01),
        'rwkv_mu': 0.5 + nrm((DEPTH, RWKV_NCH), 0.1),
        'rwkv_w0': jnp.linspace(-6.0, -1.0, GW, dtype=F32) + nrm(L2 + (GW,), 0.1),
        'rwkv_w_up': nrm(L2 + (RWKV_DECAY_RANK, GW), 0.5 * RWKV_DECAY_RANK ** -0.5),
        'rwkv_a0': nrm(L2 + (GW,), 0.1),
        'rwkv_a_up': nrm(L2 + (RWKV_A_RANK, GW), 0.5 * RWKV_A_RANK ** -0.5),
        'rwkv_g_up': nrm((DEPTH, RWKV_GATE_RANK, GW), RWKV_GATE_RANK ** -0.5),
        'rwkv_k_k': 0.85 + nrm((DEPTH, GW), 0.02),
        'rwkv_k_a': 1.0 + nrm((DEPTH, GW), 0.02),
        'rwkv_r_k': nrm((DEPTH, RWKV_HEADS, RWKV_HEAD), 0.1),
        'rwkv_ln_w': 1.0 + nrm((DEPTH, GW), 0.02),
        'rwkv_ln_b': nrm((DEPTH, GW), 0.01),
        'router_w': nrm((DEPTH, D_MODEL, N_EXPERTS), D_MODEL ** -0.5),
        'router_b': nrm((DEPTH, N_EXPERTS), 0.01),
        'exp_w1': nrm((DEPTH, N_EXPERTS, D_MODEL, 2 * D_FF), D_MODEL ** -0.5),
        'exp_b1': nrm((DEPTH, N_EXPERTS, 2 * D_FF), 0.01),
        'exp_w2': nrm((DEPTH, N_EXPERTS, D_FF, D_MODEL), D_FF ** -0.5),
        'exp_b2': nrm((DEPTH, N_EXPERTS, D_MODEL), 0.01),
        'final_norm': 1.0 + nrm((D_MODEL,), 0.02),
    }


def reference(x, c, ctx, c_ctx, w_ada, b_ada, norm1, norm2, w_in, w_out, mix_gain,
              s5_lam_re, s5_lam_im, s5_log_dt, s5_b_re, s5_b_im, s5_c_re, s5_c_im, s5_d,
              s5_w_glu, s5_b_glu, ret_decay,
              lru_conv_w, lru_conv_b, lru_lam, lru_w_r, lru_b_r, lru_w_i, lru_b_i,
              rwkv_mu, rwkv_w0, rwkv_w_up, rwkv_a0, rwkv_a_up, rwkv_g_up, rwkv_k_k, rwkv_k_a,
              rwkv_r_k, rwkv_ln_w, rwkv_ln_b,
              router_w, router_b, exp_w1, exp_b1, exp_w2, exp_b2, final_norm):
    B, L, D = x.shape
    Lc = ctx.shape[1]
    xs, cs = x, ctx
    silu_x, silu_c = jax.nn.silu(c), jax.nn.silu(c_ctx)
    blk = [GROUP_W, 5 * GROUP_W, 7 * GROUP_W]
    for l in range(DEPTH):
        last = l == DEPTH - 1
        mod_x = (silu_x @ w_ada[l] + b_ada[l]).reshape(B, 1, 6, D)
        mod_c = (silu_c @ w_ada[l] + b_ada[l]).reshape(6, D)
        mx = [mod_x[:, :, j] for j in range(6)]
        mc = [mod_c[j] for j in range(6)]
        hx = rmsnorm(xs, norm1[l]) * (1.0 + mx[1]) + mx[0]
        hc = rmsnorm(cs, norm1[l]) * (1.0 + mc[1]) + mc[0]
        pa_x, pb_x, pc_x, pd_x = jnp.split(hx @ w_in[l], blk, axis=-1)
        pa_c, pb_c, pc_c, pd_c = jnp.split(hc @ w_in[l], blk, axis=-1)
        ya_c, ya_x = s5_mixer(pa_c, pa_x, s5_lam_re[l], s5_lam_im[l], s5_log_dt[l], s5_b_re[l],
                              s5_b_im[l], s5_c_re[l], s5_c_im[l], s5_d[l], s5_w_glu[l], s5_b_glu[l])
        yb_c, yb_x = retention_mixer(pb_c, pb_x, ret_decay[l])
        yc_c, yc_x = rglru_mixer(pc_c, pc_x, lru_conv_w[l], lru_conv_b[l], lru_lam[l],
                                 lru_w_r[l], lru_b_r[l], lru_w_i[l], lru_b_i[l])
        yd_c, yd_x = rwkv7_mixer(pd_c, pd_x, rwkv_mu[l], rwkv_w0[l], rwkv_w_up[l], rwkv_a0[l],
                                 rwkv_a_up[l], rwkv_g_up[l], rwkv_k_k[l], rwkv_k_a[l], rwkv_r_k[l],
                                 rwkv_ln_w[l], rwkv_ln_b[l])
        xs = xs + (mx[2] * merge_heads([ya_x, yb_x, yc_x, yd_x], mix_gain[l], w_out[l], xs.dtype)).astype(xs.dtype)
        fx = rmsnorm(xs, norm2[l]) * (1.0 + mx[4]) + mx[3]
        if last:
            f = moe_ffn(fx.reshape(B * L, D), router_w[l], router_b[l], exp_w1[l], exp_b1[l],
                        exp_w2[l], exp_b2[l]).reshape(B, L, D)
            xs = xs + (mx[5] * f).astype(xs.dtype)
        else:
            cs = cs + (mc[2] * merge_heads([ya_c, yb_c, yc_c, yd_c], mix_gain[l], w_out[l], cs.dtype)).astype(cs.dtype)
            fc = rmsnorm(cs, norm2[l]) * (1.0 + mc[4]) + mc[3]
            tokens = jnp.concatenate([fc, fx.astype(fc.dtype)], axis=1).reshape(B * (Lc + L), D)
            f = moe_ffn(tokens, router_w[l], router_b[l], exp_w1[l], exp_b1[l],
                        exp_w2[l], exp_b2[l]).reshape(B, Lc + L, D)
            cs = cs + (mc[5] * f[:, :Lc]).astype(cs.dtype)
            xs = xs + (mx[5] * f[:, Lc:]).astype(xs.dtype)
    return rmsnorm(xs, final_norm)
```

```python
import functools
import math

import jax
import jax.numpy as jnp
from jax import lax
from jax.experimental import pallas as pl
from jax.experimental.pallas import tpu as pltpu

F32 = jnp.float32
BF16 = jnp.bfloat16

D_MODEL = 2048
SEQ = 8192
CTX_LEN = 256
N_TOK = SEQ + CTX_LEN
DEPTH = 4
GRID_W = 64
GROUP_W = 512
NORM_EPS = 1e-6
S5_CH = 16
S5_GROUPS = 32
S5_STATE = 64
RET_HEADS = 4
RET_DK = 128
RET_CHUNK = 128
ROPE_BASE = 10000.0
LRU_BLOCKS = 8
LRU_BW = 64
LRU_CONV = 4
LRU_C = 8.0
RWKV_HEAD = 64
RWKV_HEADS = 8
RWKV_RANK = 96
RWKV_GATE_RANK = 256
RWKV_LN_EPS = 64e-5
N_EXPERTS = 32
TOP_K = 4
D_FF = 896
SWIGLU_ALPHA = 1.702
SWIGLU_LIMIT = 7.0
MOE_BLOCK = 256

LANES = 128
SUBLANES = 8
VMEM_LIMIT = 56 * 1024 * 1024

N_PROJ = 5632
COL_RWKV = 0
COL_S5 = 2048
COL_RET = 2560
COL_LRU = 4608

TILE = 256
N_TILES = N_TOK // TILE
S5_SEG = 32
RW_TB = 64
RW_TILES = N_TOK // RW_TB
RW_CTX_TILES = CTX_LEN // RW_TB
RET_TILES = N_TOK // RET_CHUNK
RET_CTX_TILES = CTX_LEN // RET_CHUNK


def _params(n_axes=1):
    return pltpu.CompilerParams(
        dimension_semantics=("arbitrary",) * n_axes, vmem_limit_bytes=VMEM_LIMIT)


def _dot(a, b):
    return jnp.dot(a, b, preferred_element_type=F32)


def _split2(x):
    hi = x.astype(BF16)
    lo = (x - hi.astype(F32)).astype(BF16)
    return hi, lo


def _dot_x3(a, b):
    ah, al = _split2(a)
    bh, bl = _split2(b)
    return _dot(ah, bh) + _dot(ah, bl) + _dot(al, bh)


def _dot_sel(a, sel):
    ah, al = _split2(a)
    return _dot(ah, sel) + _dot(al, sel)


def _sel_dot(sel, a):
    ah, al = _split2(a)
    return _dot(sel, ah) + _dot(sel, al)


def _bwd_tile(s, n_ctx, n_all):
    return jnp.where(s < n_ctx, n_ctx - 1 - s, n_all - 1 - (s - n_ctx))


def _rms_rows(x):
    return x * lax.rsqrt(jnp.mean(x * x, axis=-1, keepdims=True) + NORM_EPS)


def _row_is_ctx(tile_idx, tile_rows):
    row = tile_idx * tile_rows + lax.broadcasted_iota(jnp.int32, (tile_rows, 1), 0)
    return row < CTX_LEN


def _mod_kernel(cc_ref, w_ref, b_ref, o_ref):
    s = cc_ref[...]
    s = s * jax.nn.sigmoid(s)
    o_ref[...] = _dot(s.astype(BF16), w_ref[...].astype(BF16)) + b_ref[...]


def _modulation(cc, w_ada, b_ada):
    tn = 1024
    nb = 6 * D_MODEL // tn
    return pl.pallas_call(
        _mod_kernel,
        out_shape=jax.ShapeDtypeStruct((DEPTH, SUBLANES, 6 * D_MODEL), F32),
        grid=(DEPTH, nb),
        in_specs=[
            pl.BlockSpec((SUBLANES, D_MODEL), lambda l, j: (0, 0)),
            pl.BlockSpec((None, D_MODEL, tn), lambda l, j: (l, 0, j)),
            pl.BlockSpec((None, 1, tn), lambda l, j: (l, 0, j)),
        ],
        out_specs=pl.BlockSpec((None, SUBLANES, tn), lambda l, j: (l, 0, j)),
        compiler_params=_params(2),
        name="modulation",
    )(cc, w_ada, b_ada.reshape(DEPTH, 1, 6 * D_MODEL))


def _mod_rows(mod_ref, chunk, is_ctx):
    lo, hi = chunk * D_MODEL, (chunk + 1) * D_MODEL
    return jnp.where(is_ctx, mod_ref[1:2, lo:hi], mod_ref[0:1, lo:hi])


IN_TM = 768
IN_TN = 512


def _inproj_kernel(x_ref, g_ref, mod_ref, w_ref, o_ref, h_scr):
    i, j = pl.program_id(0), pl.program_id(1)

    @pl.when(j == 0)
    def _():
        is_ctx = _row_is_ctx(i, IN_TM)
        xn = _rms_rows(x_ref[...]) * g_ref[...]
        h = xn * (1.0 + _mod_rows(mod_ref, 1, is_ctx)) + _mod_rows(mod_ref, 0, is_ctx)
        h_scr[...] = h.astype(BF16)

    o_ref[...] = _dot(h_scr[...], w_ref[...])


def _inproj(xs, norm_g, mod_l, w_in_b):
    return pl.pallas_call(
        _inproj_kernel,
        out_shape=jax.ShapeDtypeStruct((N_TOK, N_PROJ), F32),
        grid=(N_TOK // IN_TM, N_PROJ // IN_TN),
        in_specs=[
            pl.BlockSpec((IN_TM, D_MODEL), lambda i, j: (i, 0)),
            pl.BlockSpec((1, D_MODEL), lambda i, j: (0, 0)),
            pl.BlockSpec((SUBLANES, 6 * D_MODEL), lambda i, j: (0, 0)),
            pl.BlockSpec((D_MODEL, IN_TN), lambda i, j: (0, j)),
        ],
        out_specs=pl.BlockSpec((IN_TM, IN_TN), lambda i, j: (i, j)),
        scratch_shapes=[pltpu.VMEM((IN_TM, D_MODEL), BF16)],
        compiler_params=_params(2),
        name="inproj",
    )(xs, norm_g.reshape(1, D_MODEL), mod_l, w_in_b)


S5_NQ = 4
S5_QW = 2 * 8 * S5_STATE
S5_W = S5_NQ * S5_QW


def _s5_swap(h):
    half = S5_QW // 2
    parts = []
    for q in range(S5_NQ):
        parts.append(h[:, q * S5_QW + half:(q + 1) * S5_QW])
        parts.append(h[:, q * S5_QW:q * S5_QW + half])
    return jnp.concatenate(parts, axis=1)


def _s5_kernel(rev, *refs):
    if rev:
        (u_ref, yf_ref, perm_ref, permt_ref, bblk_ref, cblk_ref, a_ref, pw_ref,
         dskip_ref, wglu_ref, bglu_ref, o_ref, carry_scr, bu_scr, hl_scr, hs_scr) = refs
    else:
        (u_ref, perm_ref, permt_ref, bblk_ref, cblk_ref, a_ref, pw_ref,
         o_ref, carry_scr, bu_scr, hl_scr, hs_scr) = refs

    @pl.when(pl.program_id(0) == 0)
    def _():
        carry_scr[...] = jnp.zeros_like(carry_scr)

    u = u_ref[...]
    up = _dot(perm_ref[...], u.astype(BF16)).astype(BF16)
    for q in range(S5_NQ):
        bu = _dot(up[:, q * LANES:(q + 1) * LANES], bblk_ref[q])
        bu_scr[:, :, q * S5_QW:(q + 1) * S5_QW] = bu.reshape(S5_SEG, SUBLANES, S5_QW)

    a1, a2 = a_ref[0:1, :], a_ref[1:2, :]
    at1, at2 = a_ref[2:3, :], a_ref[3:4, :]

    def step(s, h):
        p = S5_SEG - 1 - s if rev else s
        h = a1 * h + a2 * _s5_swap(h) + bu_scr[p]
        hl_scr[p] = h
        return h

    ends = lax.fori_loop(0, S5_SEG, step, jnp.zeros((SUBLANES, S5_W), F32))

    c = carry_scr[...]
    for j in (range(SUBLANES - 1, -1, -1) if rev else range(SUBLANES)):
        hs_scr[j:j + 1, :] = c
        c = ends[j:j + 1, :] + at1 * c + at2 * _s5_swap(c)
    carry_scr[...] = c

    hs = hs_scr[...]
    hsw = _s5_swap(hs)

    def fix(p8, carry):
        base = pl.multiple_of(p8 * SUBLANES, SUBLANES)
        pw1 = pw_ref[0, pl.ds(base, SUBLANES), :]
        pw2 = pw_ref[1, pl.ds(base, SUBLANES), :]
        for tt in range(SUBLANES):
            p = base + tt
            hl_scr[p] = hl_scr[p] + pw1[tt:tt + 1, :] * hs + pw2[tt:tt + 1, :] * hsw
        return carry

    lax.fori_loop(0, S5_SEG // SUBLANES, fix, 0)

    ys = []
    for q in range(S5_NQ):
        hq = hl_scr[:, :, q * S5_QW:(q + 1) * S5_QW].reshape(TILE, S5_QW)
        ys.append(_dot(hq.astype(BF16), cblk_ref[q]))
    y = _sel_dot(permt_ref[...], jnp.concatenate(ys, axis=1))

    if rev:
        y = y + yf_ref[...] + dskip_ref[...] * u
        y = jax.nn.gelu(y)
        gate = _dot(y.astype(BF16), wglu_ref[...]) + bglu_ref[...]
        o_ref[...] = y * jax.nn.sigmoid(gate)
    else:
        o_ref[...] = y


def _s5_tables(lam_re, lam_im, log_dt, b_re, b_im, c_re, c_im, rev):
    dt = jnp.exp(log_dt.astype(F32))[:, None]
    lr = jnp.minimum(lam_re.astype(F32), -1e-4)
    li = lam_im.astype(F32)

    def a_pow(k):
        mag = jnp.exp(k * lr * dt)
        return mag * jnp.cos(k * li * dt), mag * jnp.sin(k * li * dt)

    def cols(re, im):
        lead = re.shape[:-2]
        re = re.reshape(lead + (S5_NQ, 8 * S5_STATE))
        im = im.reshape(lead + (S5_NQ, 8 * S5_STATE))
        return jnp.concatenate([re, im], axis=-1).reshape(lead + (S5_W,))

    ab_re, ab_im = a_pow(1.0)
    den = lr * lr + li * li
    f_re = ((ab_re - 1.0) * lr + ab_im * li) / den
    f_im = (ab_im * lr - (ab_re - 1.0) * li) / den
    bb_re = f_re[..., None] * b_re - f_im[..., None] * b_im
    bb_im = f_re[..., None] * b_im + f_im[..., None] * b_re
    at_re, at_im = a_pow(float(S5_SEG))
    a_tab = jnp.stack([cols(ab_re, ab_re), cols(-ab_im, ab_im),
                       cols(at_re, at_re), cols(-at_im, at_im)])
    ks = jnp.arange(S5_SEG, dtype=F32)
    ks = (S5_SEG - ks) if rev else (ks + 1.0)
    pk_re, pk_im = a_pow(ks[:, None, None])
    pw = jnp.stack([cols(pk_re, pk_re), cols(-pk_im, pk_im)])

    eye = jnp.eye(8, dtype=F32)

    def bdiag(m):
        a, b = m.shape[1:]
        m4 = m.reshape(S5_NQ, 8, a, b)
        return jnp.einsum('qgab,gh->qgahb', m4, eye).reshape(S5_NQ, 8 * a, 8 * b)

    bblk = jnp.concatenate([bdiag(jnp.swapaxes(bb_re, 1, 2)),
                            bdiag(jnp.swapaxes(bb_im, 1, 2))], axis=2).astype(BF16)
    cblk = jnp.concatenate([bdiag(jnp.swapaxes(c_re.astype(F32), 1, 2)),
                            bdiag(-jnp.swapaxes(c_im.astype(F32), 1, 2))], axis=1).astype(BF16)
    return a_tab, pw, bblk, cblk


def _s5_perm():
    r = jnp.arange(TILE)
    src = (r % SUBLANES) * S5_SEG + r // SUBLANES
    perm = (src[:, None] == jnp.arange(TILE)[None, :]).astype(BF16)
    return perm, perm.T


def _s5_mixer(proj, tabs_f, tabs_b, d_skip, w_glu, b_glu):
    perm, permt = _s5_perm()
    col = COL_S5 // GROUP_W

    def const(shape):
        return pl.BlockSpec(shape, lambda s: (0,) * len(shape))

    tab_specs = [const((TILE, TILE)), const((TILE, TILE)),
                 const((S5_NQ, LANES, S5_QW)), const((S5_NQ, S5_QW, LANES)),
                 const((4, S5_W)), const((2, S5_SEG, S5_W))]
    scratch = [pltpu.VMEM((1, S5_W), F32),
               pltpu.VMEM((S5_SEG, SUBLANES, S5_W), F32),
               pltpu.VMEM((S5_SEG, SUBLANES, S5_W), F32),
               pltpu.VMEM((SUBLANES, S5_W), F32)]
    out_shape = jax.ShapeDtypeStruct((N_TOK, GROUP_W), F32)

    a_tab, pw, bblk, cblk = tabs_f
    yf = pl.pallas_call(
        functools.partial(_s5_kernel, False),
        out_shape=out_shape, grid=(N_TILES,),
        in_specs=[pl.BlockSpec((TILE, GROUP_W), lambda s: (s, col))] + tab_specs,
        out_specs=pl.BlockSpec((TILE, GROUP_W), lambda s: (s, 0)),
        scratch_shapes=scratch, compiler_params=_params(), name="s5_fwd",
    )(proj, perm, permt, bblk, cblk, a_tab, pw)

    a_tab, pw, bblk, cblk = tabs_b
    bt = lambda s: _bwd_tile(s, 1, N_TILES)
    return pl.pallas_call(
        functools.partial(_s5_kernel, True),
        out_shape=out_shape, grid=(N_TILES,),
        in_specs=[pl.BlockSpec((TILE, GROUP_W), lambda s: (bt(s), col)),
                  pl.BlockSpec((TILE, GROUP_W), lambda s: (bt(s), 0))] + tab_specs
                 + [const((1, GROUP_W)), const((GROUP_W, GROUP_W)), const((1, GROUP_W))],
        out_specs=pl.BlockSpec((TILE, GROUP_W), lambda s: (bt(s), 0)),
        scratch_shapes=scratch, compiler_params=_params(), name="s5_bwd",
    )(proj, yf, perm, permt, bblk, cblk, a_tab, pw,
      d_skip.reshape(1, GROUP_W), w_glu.astype(BF16), b_glu.reshape(1, GROUP_W))


def _rope_shuffle(x):
    lane = lax.broadcasted_iota(jnp.int32, x.shape, 1)
    return jnp.where(lane % 64 < 32, pltpu.roll(x, LANES - 32, 1), pltpu.roll(x, 32, 1))


def _ret_kernel(rev, *refs):
    if rev:
        (q_ref, k_ref, v_ref, g_ref, of_ref, cos_ref, sin_ref, dintra_ref, dq_ref, dk_ref,
         dc_ref, o_ref, s_scr) = refs
    else:
        (q_ref, k_ref, v_ref, cos_ref, sin_ref, dintra_ref, dq_ref, dk_ref,
         dc_ref, o_ref, s_scr) = refs

    @pl.when(pl.program_id(0) == 0)
    def _():
        s_scr[...] = jnp.zeros_like(s_scr)

    cos, sin = cos_ref[...], sin_ref[...]
    outs = []
    for h in range(RET_HEADS):
        sl = slice(h * RET_DK, (h + 1) * RET_DK)
        qh = q_ref[:, sl]
        kh = k_ref[:, sl] * (RET_DK ** -0.5)
        qh = qh * cos + _rope_shuffle(qh) * sin
        kh = kh * cos + _rope_shuffle(kh) * sin
        qb, kb, vb = qh.astype(BF16), kh.astype(BF16), v_ref[:, sl].astype(BF16)
        sc = lax.dot_general(qb, kb, (((1,), (1,)), ((), ())),
                             preferred_element_type=F32) * dintra_ref[h]
        s_old = s_scr[h]
        o = _dot(sc.astype(BF16), vb) + _dot(qb, s_old.astype(BF16)) * dq_ref[h]
        kt = (kh * dk_ref[h]).T.astype(BF16)
        s_scr[h] = s_old * dc_ref[h] + _dot(kt, vb)
        if rev:
            o = o + of_ref[:, sl]
            gh = g_ref[:, sl]
            o = o * lax.rsqrt(jnp.mean(o * o, axis=-1, keepdims=True) + NORM_EPS)
            o = o * (gh * jax.nn.sigmoid(gh))
        outs.append(o)
    o_ref[...] = jnp.concatenate(outs, axis=1)


def _ret_tables(decay):
    log_g = -jnp.exp(decay.astype(F32))
    idx = jnp.arange(RET_CHUNK, dtype=F32)
    rel = idx[:, None] - idx[None, :]
    c = float(RET_CHUNK)
    lf = log_g[0][:, None, None]
    lb = log_g[1][:, None, None]
    d_f = jnp.where((rel >= 0)[None], jnp.exp(jnp.maximum(rel, 0.0)[None] * lf), 0.0)
    d_b = jnp.where((rel < 0)[None], jnp.exp(jnp.maximum(-rel, 0.0)[None] * lb), 0.0)
    ones = jnp.ones((1, 1, RET_DK), F32)
    dq_f = jnp.exp((idx + 1.0)[None, :, None] * lf) * ones
    dk_f = jnp.exp((c - 1.0 - idx)[None, :, None] * lf) * ones
    dq_b = jnp.exp((c - idx)[None, :, None] * lb) * ones
    dk_b = jnp.exp(idx[None, :, None] * lb) * ones
    dc_f = jnp.exp(c * lf) * ones
    dc_b = jnp.exp(c * lb) * ones
    return (d_f, dq_f, dk_f, dc_f), (d_b, dq_b, dk_b, dc_b)


def _rope_tables():
    m = 32
    inv = ROPE_BASE ** (-jnp.arange(m, dtype=F32) / m)
    t = jnp.arange(SEQ)
    ang_r = (t // GRID_W).astype(F32)[:, None] * inv[None, :]
    ang_c = (t % GRID_W).astype(F32)[:, None] * inv[None, :]
    cos = jnp.concatenate([jnp.cos(ang_r)] * 2 + [jnp.cos(ang_c)] * 2, axis=1)
    sin = jnp.concatenate([-jnp.sin(ang_r), jnp.sin(ang_r), -jnp.sin(ang_c), jnp.sin(ang_c)], axis=1)
    cos = jnp.concatenate([jnp.ones((CTX_LEN, RET_DK), F32), cos], axis=0)
    sin = jnp.concatenate([jnp.zeros((CTX_LEN, RET_DK), F32), sin], axis=0)
    return cos, sin


def _ret_mixer(proj, rope, tabs_f, tabs_b):
    cos, sin = rope
    c0 = COL_RET // GROUP_W

    def const(shape):
        return pl.BlockSpec(shape, lambda s: (0,) * len(shape))

    def tab_specs():
        return [const((RET_HEADS, RET_CHUNK, RET_CHUNK)), const((RET_HEADS, RET_CHUNK, RET_DK)),
                const((RET_HEADS, RET_CHUNK, RET_DK)), const((RET_HEADS, 1, RET_DK))]

    out_shape = jax.ShapeDtypeStruct((N_TOK, GROUP_W), F32)
    scratch = [pltpu.VMEM((RET_HEADS, RET_DK, RET_DK), F32)]

    def tok(cb, f):
        return pl.BlockSpec((RET_CHUNK, GROUP_W), lambda s: (f(s), cb))

    def rope_spec(f):
        return pl.BlockSpec((RET_CHUNK, RET_DK), lambda s: (f(s), 0))

    ident = lambda s: s
    of = pl.pallas_call(
        functools.partial(_ret_kernel, False),
        out_shape=out_shape, grid=(RET_TILES,),
        in_specs=[tok(c0, ident), tok(c0 + 1, ident), tok(c0 + 2, ident),
                  rope_spec(ident), rope_spec(ident)] + tab_specs(),
        out_specs=tok(0, ident),
        scratch_shapes=scratch, compiler_params=_params(), name="ret_fwd",
    )(proj, proj, proj, cos, sin, *tabs_f)

    bt = lambda s: _bwd_tile(s, RET_CTX_TILES, RET_TILES)
    return pl.pallas_call(
        functools.partial(_ret_kernel, True),
        out_shape=out_shape, grid=(RET_TILES,),
        in_specs=[tok(c0, bt), tok(c0 + 1, bt), tok(c0 + 2, bt), tok(c0 + 3, bt), tok(0, bt),
                  rope_spec(bt), rope_spec(bt)] + tab_specs(),
        out_specs=tok(0, bt),
        scratch_shapes=scratch, compiler_params=_params(), name="ret_bwd",
    )(proj, proj, proj, proj, of, cos, sin, *tabs_b)


def _lru_kernel(rev, *refs):
    if rev:
        (pv_ref, x_ref, nx_ref, gate_ref, hf_ref, cw_ref, cb_ref, sp_ref, wr_ref, br_ref,
         wi_ref, bi_ref, o_ref, carry_scr) = refs
    else:
        (pv_ref, x_ref, nx_ref, cw_ref, cb_ref, sp_ref, wr_ref, br_ref,
         wi_ref, bi_ref, o_ref, carry_scr) = refs
    s = pl.program_id(0)
    tile = _bwd_tile(s, 1, N_TILES) if rev else s

    @pl.when(s == 0)
    def _():
        carry_scr[...] = jnp.zeros_like(carry_scr)

    has_prev = jnp.logical_and(tile != 0, tile != 1).astype(F32)
    has_next = jnp.logical_and(tile != 0, tile != N_TILES - 1).astype(F32)
    x = x_ref[...]
    pv = pv_ref[...] * has_prev
    nx = nx_ref[...] * has_next
    row = lax.broadcasted_iota(jnp.int32, (TILE, GROUP_W), 0)
    xm1 = jnp.where(row == 0, pv[7:8, :], pltpu.roll(x, 1, 0))
    xm2 = jnp.where(row == 0, pv[6:7, :], jnp.where(row == 1, pv[7:8, :], pltpu.roll(x, 2, 0)))
    xp1 = jnp.where(row == TILE - 1, nx[0:1, :], pltpu.roll(x, TILE - 1, 0))
    xc = (xm2 * cw_ref[0:1, :] + xm1 * cw_ref[1:2, :] + x * cw_ref[2:3, :]
          + xp1 * cw_ref[3:4, :] + cb_ref[...])

    r = jax.nn.sigmoid(_dot_x3(xc, wr_ref[...]) + br_ref[...])
    ig = jax.nn.sigmoid(_dot_x3(xc, wi_ref[...]) + bi_ref[...])
    log_a = -LRU_C * r * sp_ref[...]
    a = jnp.exp(log_a)
    b = jnp.sqrt(1.0 - jnp.exp(2.0 * log_a)) * (ig * xc)

    sh = 1
    while sh < TILE:
        if rev:
            keep = row < TILE - sh
            a_s = jnp.where(keep, pltpu.roll(a, TILE - sh, 0), 1.0)
            b_s = jnp.where(keep, pltpu.roll(b, TILE - sh, 0), 0.0)
        else:
            keep = row >= sh
            a_s = jnp.where(keep, pltpu.roll(a, sh, 0), 1.0)
            b_s = jnp.where(keep, pltpu.roll(b, sh, 0), 0.0)
        b = a * b_s + b
        a = a * a_s
        sh *= 2
    h = b + a * carry_scr[...]
    carry_scr[...] = h[0:1, :] if rev else h[TILE - 1:TILE, :]
    if rev:
        o_ref[...] = (h + hf_ref[...]) * jax.nn.gelu(gate_ref[...])
    else:
        o_ref[...] = h


def _blockdiag(w):
    h, a, b = w.shape
    return jnp.einsum('hab,hg->hagb', w, jnp.eye(h, dtype=w.dtype)).reshape(h * a, h * b)


def _lru_mixer(proj, conv_w, conv_b, lam, w_r, b_r, w_i, b_i):
    cx = COL_LRU // GROUP_W
    rows8 = TILE // SUBLANES
    last8 = N_TOK // SUBLANES - 1

    def const(shape):
        return pl.BlockSpec(shape, lambda s: (0,) * len(shape))

    def specs(f, d):
        return dict(
            pv=pl.BlockSpec((SUBLANES, GROUP_W), lambda s: (jnp.maximum(f(s) * rows8 - 1, 0), cx)),
            x=pl.BlockSpec((TILE, GROUP_W), lambda s: (f(s), cx)),
            nx=pl.BlockSpec((SUBLANES, GROUP_W), lambda s: (jnp.minimum((f(s) + 1) * rows8, last8), cx)),
            gate=pl.BlockSpec((TILE, GROUP_W), lambda s: (f(s), cx + 1)),
            out=pl.BlockSpec((TILE, GROUP_W), lambda s: (f(s), 0)),
        )

    def dir_params(d):
        return (conv_w, conv_b.reshape(1, GROUP_W),
                jax.nn.softplus(-lam[d].astype(F32)).reshape(1, GROUP_W),
                _blockdiag(w_r[d]), b_r[d].reshape(1, GROUP_W),
                _blockdiag(w_i[d]), b_i[d].reshape(1, GROUP_W))

    par_specs = [const((LRU_CONV, GROUP_W)), const((1, GROUP_W)), const((1, GROUP_W)),
                 const((GROUP_W, GROUP_W)), const((1, GROUP_W)),
                 const((GROUP_W, GROUP_W)), const((1, GROUP_W))]
    out_shape = jax.ShapeDtypeStruct((N_TOK, GROUP_W), F32)
    scratch = [pltpu.VMEM((1, GROUP_W), F32)]

    sp = specs(lambda s: s, 0)
    hf = pl.pallas_call(
        functools.partial(_lru_kernel, False),
        out_shape=out_shape, grid=(N_TILES,),
        in_specs=[sp['pv'], sp['x'], sp['nx']] + par_specs,
        out_specs=sp['out'], scratch_shapes=scratch, compiler_params=_params(), name="lru_fwd",
    )(proj, proj, proj, *dir_params(0))

    sp = specs(lambda s: _bwd_tile(s, 1, N_TILES), 1)
    return pl.pallas_call(
        functools.partial(_lru_kernel, True),
        out_shape=out_shape, grid=(N_TILES,),
        in_specs=[sp['pv'], sp['x'], sp['nx'], sp['gate'], sp['out']] + par_specs,
        out_specs=sp['out'], scratch_shapes=scratch, compiler_params=_params(), name="lru_bwd",
    )(proj, proj, proj, proj, hf, *dir_params(1))


RW_IN = 2048
RW_HALO = 64


def _head_ones():
    h = jnp.arange(GROUP_W) // RWKV_HEAD
    return (h[:, None] == h[None, :]).astype(BF16)


def _rwkv_prep_kernel(pv_ref, cur_ref, nx_ref, mu_ref, kk_ref, ka_ref, rk_ref, w0_ref, wup_ref,
                      a0_ref, aup_ref, gup_ref, ones_ref,
                      r_o, v_o, kkn_o, bonus_o, g_o, w_o, kd_o, b_o, ext_scr, z_scr):
    i = pl.program_id(0)
    is_ctx = i == 0
    ext_scr[0:RW_HALO, :] = pv_ref[...]
    ext_scr[RW_HALO:RW_HALO + TILE, :] = cur_ref[...]
    ext_scr[RW_HALO + TILE:, :] = nx_ref[...]

    row = lax.broadcasted_iota(jnp.int32, (TILE, LANES), 0)
    lane = lax.broadcasted_iota(jnp.int32, (TILE, LANES), 1)
    c4 = lane % 4
    one = jnp.ones((TILE, LANES), F32)
    zero = jnp.zeros((TILE, LANES), F32)

    def mask(c):
        return jnp.where(c, one, zero)

    up_rows = jnp.where(i == 1, mask(row >= RW_HALO), one)
    dn_rows = jnp.where(i == N_TILES - 1, mask(row < TILE - RW_HALO), one)
    m_up = jnp.where(is_ctx, zero, up_rows * mask(c4 == 0))
    m_dn = jnp.where(is_ctx, zero, dn_rows * mask(c4 == 1))
    m_lt = jnp.where(is_ctx, mask(row >= 1) * mask(c4 % 2 == 0),
                     mask(row % GRID_W != 0) * mask(c4 == 2))
    m_rt = jnp.where(is_ctx, mask(row <= TILE - 2) * mask(c4 % 2 == 1),
                     mask(row % GRID_W != GRID_W - 1) * mask(c4 == 3))

    for cb in range(RW_IN // LANES):
        sl = slice(cb * LANES, (cb + 1) * LANES)
        p = ext_scr[RW_HALO:RW_HALO + TILE, sl]
        shifted = (ext_scr[0:TILE, sl] * m_up
                   + ext_scr[2 * RW_HALO:2 * RW_HALO + TILE, sl] * m_dn
                   + ext_scr[RW_HALO - 1:RW_HALO - 1 + TILE, sl] * m_lt
                   + ext_scr[RW_HALO + 1:RW_HALO + 1 + TILE, sl] * m_rt)
        z_scr[:, sl] = p + (shifted - p) * mu_ref[:, sl]

    gw = GROUP_W
    r = z_scr[:, 0:gw]
    k = z_scr[:, gw:2 * gw]
    v = z_scr[:, 2 * gw:3 * gw]
    wc = z_scr[:, 3 * gw:3 * gw + LANES]
    ac = z_scr[:, 3 * gw + LANES:3 * gw + 2 * LANES]
    gc = z_scr[:, 3 * gw + 2 * LANES:]
    ones = ones_ref[...]

    r_o[...] = r
    v_o[...] = v
    g_o[...] = _dot(jax.nn.sigmoid(gc).astype(BF16), gup_ref[...])
    kk = k * kk_ref[...]
    ss = _dot_sel(kk * kk, ones)
    kk = kk * lax.rsqrt(jnp.maximum(ss, 1e-12))
    kkn_o[...] = kk
    bonus_o[...] = _dot_sel(r * k * rk_ref[...], ones) * v
    tw = jnp.tanh(wc)
    for d in range(2):
        w_log = -jax.nn.softplus(-(w0_ref[d] + _dot_x3(tw, wup_ref[d]))) - 0.5
        w_o[d] = jnp.exp(-jnp.exp(w_log))
        a = jax.nn.sigmoid(a0_ref[d] + _dot_x3(ac, aup_ref[d]))
        kd_o[d] = k * (1.0 + (a - 1.0) * ka_ref[...])
        b_o[d] = kk * a


def _rwkv_prep(proj, mu, k_k, k_a, r_k, w0, w_up, a0, a_up, g_up):
    blk64 = TILE // RW_HALO
    last64 = N_TOK // RW_HALO - 1

    def const(shape):
        return pl.BlockSpec(shape, lambda i: (0,) * len(shape))

    tok = pl.BlockSpec((TILE, GROUP_W), lambda i: (i, 0))
    tok2 = pl.BlockSpec((2, TILE, GROUP_W), lambda i: (0, i, 0))
    one = jax.ShapeDtypeStruct((N_TOK, GROUP_W), F32)
    two = jax.ShapeDtypeStruct((2, N_TOK, GROUP_W), F32)
    pad = LANES - RWKV_RANK

    def pad_rows(w):
        return jnp.pad(w.astype(F32), ((0, 0), (0, pad), (0, 0)))

    def pad_mu(m):
        z = jnp.zeros((pad,), F32)
        g3 = 3 * GROUP_W
        return jnp.concatenate([m[:g3], m[g3:g3 + RWKV_RANK], z,
                                m[g3 + RWKV_RANK:g3 + 2 * RWKV_RANK], z,
                                m[g3 + 2 * RWKV_RANK:]]).reshape(1, RW_IN)

    return pl.pallas_call(
        _rwkv_prep_kernel,
        out_shape=(one, one, one, one, one, two, two, two),
        grid=(N_TILES,),
        in_specs=[
            pl.BlockSpec((RW_HALO, RW_IN), lambda i: (jnp.maximum(i * blk64 - 1, 0), 0)),
            pl.BlockSpec((TILE, RW_IN), lambda i: (i, 0)),
            pl.BlockSpec((RW_HALO, RW_IN), lambda i: (jnp.minimum((i + 1) * blk64, last64), 0)),
            const((1, RW_IN)), const((1, GROUP_W)), const((1, GROUP_W)), const((1, GROUP_W)),
            const((2, 1, GROUP_W)), const((2, LANES, GROUP_W)),
            const((2, 1, GROUP_W)), const((2, LANES, GROUP_W)),
            const((RWKV_GATE_RANK, GROUP_W)), const((GROUP_W, GROUP_W)),
        ],
        out_specs=(tok, tok, tok, tok, tok, tok2, tok2, tok2),
        scratch_shapes=[pltpu.VMEM((TILE + 2 * RW_HALO, RW_IN), F32),
                        pltpu.VMEM((TILE, RW_IN), F32)],
        compiler_params=_params(), name="rwkv_prep",
    )(proj, proj, proj, pad_mu(mu.astype(F32)), k_k.reshape(1, GROUP_W), k_a.reshape(1, GROUP_W),
      r_k.reshape(1, GROUP_W), w0.reshape(2, 1, GROUP_W), pad_rows(w_up),
      a0.reshape(2, 1, GROUP_W), pad_rows(a_up), g_up.astype(BF16), _head_ones())


def _rwkv_scan_kernel(rf, vf, kf, wf, kdf, bf, rb, vb, kb, wb, kdb, bb, eye_ref, bd_ref,
                      yf_o, yb_o, s_scr, vexp_scr, yr_scr):
    @pl.when(pl.program_id(0) == 0)
    def _():
        s_scr[...] = jnp.zeros_like(s_scr)

    n_pair = GROUP_W // LANES
    dirs = ((rf, vf, kf, wf, kdf, bf), (rb, vb, kb, wb, kdb, bb))
    eye_b = eye_ref[...].astype(BF16)
    bd = bd_ref[...]

    for d in range(2):
        v_ref = dirs[d][1]
        for hq in range(GROUP_W // RW_BW):
            sl = slice(hq * RW_BW, (hq + 1) * RW_BW)
            vp = v_ref[:, sl].astype(BF16)
            lhs = (vp[:, None, :] * eye_b[None]).reshape(RW_TB * RWKV_HEAD, RW_BW)
            vexp_scr[d, :, :, sl] = _dot(lhs, bd).reshape(RW_TB, RWKV_HEAD, RW_BW)

    lane = lax.broadcasted_iota(jnp.int32, (RWKV_HEAD, LANES), 1)
    low = lane < RWKV_HEAD

    n_grp = RW_TB // SUBLANES

    def step(sg, carry):
        bases = (pl.multiple_of(sg * SUBLANES, SUBLANES),
                 pl.multiple_of((n_grp - 1 - sg) * SUBLANES, SUBLANES))
        rows = [[[ref[pl.ds(bases[d], SUBLANES), pr * LANES:(pr + 1) * LANES]
                  for ref in (dirs[d][0],) + dirs[d][2:]]
                 for pr in range(n_pair)] for d in range(2)]
        for tt in range(SUBLANES):
            for d in range(2):
                ti = tt if d == 0 else SUBLANES - 1 - tt
                t = bases[d] + ti
                for pr in range(n_pair):
                    sl = slice(pr * LANES, (pr + 1) * LANES)
                    r_t, kk_t, w_t, kd_t, b_t = [a[ti:ti + 1, :] for a in rows[d][pr]]
                    st = s_scr[d, :, sl]
                    if d == 1:
                        yr_scr[d, t, :, sl] = (st * r_t).astype(BF16)
                    x = st * kk_t
                    tot = jnp.sum(x, axis=1, keepdims=True)
                    lo = jnp.sum(jnp.where(low, x, 0.0), axis=1, keepdims=True)
                    sa = jnp.where(low, lo, tot - lo)
                    st = st * w_t - sa * b_t + vexp_scr[d, t, :, sl] * kd_t
                    s_scr[d, :, sl] = st
                    if d == 0:
                        yr_scr[d, t, :, sl] = (st * r_t).astype(BF16)
        return carry

    lax.fori_loop(0, n_grp, step, 0)

    eye_f = eye_ref[...]
    for d, y_o in enumerate((yf_o, yb_o)):
        for hq in range(GROUP_W // RW_BW):
            sl = slice(hq * RW_BW, (hq + 1) * RW_BW)
            ys = _dot(yr_scr[d, :, :, sl].reshape(RW_TB * RWKV_HEAD, RW_BW), bd)
            y_o[:, sl] = jnp.sum(ys.reshape(RW_TB, RWKV_HEAD, RW_BW) * eye_f[None], axis=1)


RW_BW = 256


def _rwkv_scan(r, v, kk, w, kd, b):
    i64 = jnp.arange(RWKV_HEAD)
    lw = jnp.arange(RW_BW)
    eye2 = (i64[:, None] == (lw % RWKV_HEAD)[None, :]).astype(F32)
    bd = ((lw // RWKV_HEAD)[:, None] == (lw // RWKV_HEAD)[None, :]).astype(BF16)

    ft = lambda s: s
    bt = lambda s: _bwd_tile(s, RW_CTX_TILES, RW_TILES)

    def one(f):
        return pl.BlockSpec((RW_TB, GROUP_W), lambda s: (f(s), 0))

    def two(f, d):
        return pl.BlockSpec((None, RW_TB, GROUP_W), lambda s: (d, f(s), 0))

    out = jax.ShapeDtypeStruct((N_TOK, GROUP_W), F32)
    return pl.pallas_call(
        _rwkv_scan_kernel,
        out_shape=(out, out),
        grid=(RW_TILES,),
        in_specs=[one(ft), one(ft), one(ft), two(ft, 0), two(ft, 0), two(ft, 0),
                  one(bt), one(bt), one(bt), two(bt, 1), two(bt, 1), two(bt, 1),
                  pl.BlockSpec((RWKV_HEAD, RW_BW), lambda s: (0, 0)),
                  pl.BlockSpec((RW_BW, RW_BW), lambda s: (0, 0))],
        out_specs=(one(ft), one(bt)),
        scratch_shapes=[pltpu.VMEM((2, RWKV_HEAD, GROUP_W), F32),
                        pltpu.VMEM((2, RW_TB, RWKV_HEAD, GROUP_W), F32),
                        pltpu.VMEM((2, RW_TB, RWKV_HEAD, GROUP_W), BF16)],
        compiler_params=_params(), name="rwkv_scan",
    )(r, v, kk, w, kd, b, r, v, kk, w, kd, b, eye2, bd)


def _merge_kernel(ya_ref, yb_ref, yc_ref, ydf_ref, ydb_ref, bonus_ref, g_ref, lnw_ref, lnb_ref,
                  ones_ref, gain_ref, wout_ref, xs_ref, mod_ref, n2_ref, rw_ref, rb_ref,
                  xo_ref, fx_ref, te_ref, tg_ref):
    i = pl.program_id(0)
    is_ctx = _row_is_ctx(i, TILE)
    ones = ones_ref[...]
    inv = 1.0 / RWKV_HEAD
    yd = ydf_ref[...] + ydb_ref[...]
    mean = _dot_sel(yd, ones) * inv
    dl = yd - mean
    var = _dot_sel(dl * dl, ones) * inv
    yd = (dl * lax.rsqrt(var + RWKV_LN_EPS) * lnw_ref[...] + lnb_ref[...] + bonus_ref[...]) * g_ref[...]

    parts = []
    for gi, y in enumerate((ya_ref[...], yb_ref[...], yc_ref[...], yd)):
        parts.append((_rms_rows(y) * gain_ref[:, gi * GROUP_W:(gi + 1) * GROUP_W]).astype(BF16))
    m = _dot(jnp.concatenate(parts, axis=1), wout_ref[...])
    xs = xs_ref[...] + _mod_rows(mod_ref, 2, is_ctx) * m
    xo_ref[...] = xs

    fx = _rms_rows(xs) * n2_ref[...]
    fx = fx * (1.0 + _mod_rows(mod_ref, 4, is_ctx)) + _mod_rows(mod_ref, 3, is_ctx)
    fx_ref[...] = fx

    logits = _dot_x3(fx, rw_ref[...]) + rb_ref[...]
    lane = lax.broadcasted_iota(jnp.int32, logits.shape, 1)
    vals = logits
    tops, idxs = [], []
    for _ in range(TOP_K):
        mx = jnp.max(vals, axis=-1, keepdims=True)
        ix = jnp.min(jnp.where(vals == mx, lane, LANES), axis=-1, keepdims=True)
        tops.append(mx)
        idxs.append(ix)
        vals = jnp.where(lane == ix, -jnp.inf, vals)
    es = [jnp.exp(t - tops[0]) for t in tops]
    den = es[0] + es[1] + es[2] + es[3]
    te = jnp.zeros(logits.shape, jnp.int32)
    tg = jnp.zeros(logits.shape, F32)
    for kx in range(TOP_K):
        te = jnp.where(lane == kx, idxs[kx], te)
        tg = jnp.where(lane == kx, es[kx] / den, tg)
    te_ref[...] = te
    tg_ref[...] = tg


def _merge(ya, yb, yc, ydf, ydb, bonus, g, ln_w, ln_b, gain, w_out_b, xs, mod_l, norm2,
           router_w, router_b):
    def const(shape):
        return pl.BlockSpec(shape, lambda i: (0,) * len(shape))

    grp = pl.BlockSpec((TILE, GROUP_W), lambda i: (i, 0))
    full = pl.BlockSpec((TILE, D_MODEL), lambda i: (i, 0))
    lanes = pl.BlockSpec((TILE, LANES), lambda i: (i, 0))
    rw = jnp.pad(router_w.astype(F32), ((0, 0), (0, LANES - N_EXPERTS)))
    rb = jnp.concatenate([router_b.astype(F32), jnp.full((LANES - N_EXPERTS,), -1e30, F32)])
    return pl.pallas_call(
        _merge_kernel,
        out_shape=(jax.ShapeDtypeStruct((N_TOK, D_MODEL), F32),
                   jax.ShapeDtypeStruct((N_TOK, D_MODEL), F32),
                   jax.ShapeDtypeStruct((N_TOK, LANES), jnp.int32),
                   jax.ShapeDtypeStruct((N_TOK, LANES), F32)),
        grid=(N_TILES,),
        in_specs=[grp] * 7 + [const((1, GROUP_W)), const((1, GROUP_W)), const((GROUP_W, GROUP_W)),
                              const((1, D_MODEL)), const((D_MODEL, D_MODEL)), full,
                              const((SUBLANES, 6 * D_MODEL)), const((1, D_MODEL)),
                              const((D_MODEL, LANES)), const((1, LANES))],
        out_specs=(full, full, lanes, lanes),
        compiler_params=_params(), name="merge",
    )(ya, yb, yc, ydf, ydb, bonus, g, ln_w.reshape(1, GROUP_W), ln_b.reshape(1, GROUP_W),
      _head_ones(), gain.reshape(1, D_MODEL), w_out_b, xs, mod_l, norm2.reshape(1, D_MODEL),
      rw, rb.reshape(1, LANES))


N_SLOTS = N_TOK * TOP_K
MOE_NBLK = N_SLOTS // MOE_BLOCK + N_EXPERTS
MOE_ROWS = MOE_NBLK * MOE_BLOCK
MOE_OUT_ROWS = N_SLOTS + 2 * MOE_BLOCK


def _route(top_e, top_g):
    flat_e = top_e.reshape(N_SLOTS)
    onehot = (flat_e[:, None] == jnp.arange(N_EXPERTS, dtype=jnp.int32)[None, :]).astype(jnp.int32)
    csum = jnp.cumsum(onehot, axis=0)
    rank = jnp.sum(onehot * csum, axis=1) - 1
    counts = csum[-1]
    padded = (counts + MOE_BLOCK - 1) // MOE_BLOCK * MOE_BLOCK
    pends = jnp.cumsum(padded)
    dest = (pends - padded)[flat_e] + rank
    slot = jnp.arange(N_SLOTS, dtype=jnp.int32)
    rows_tok = jnp.zeros((MOE_ROWS,), jnp.int32).at[dest].set(slot // TOP_K)
    row = jnp.arange(MOE_ROWS + MOE_BLOCK, dtype=jnp.int32) - MOE_BLOCK
    dump = N_SLOTS + ((row // MOE_BLOCK) % 2) * MOE_BLOCK + row % MOE_BLOCK
    rows_dst = dump.at[dest + MOE_BLOCK].set(slot)
    rows_gate = jnp.zeros((MOE_ROWS,), F32).at[dest].set(top_g.reshape(N_SLOTS))
    blk_row0 = jnp.arange(MOE_NBLK, dtype=jnp.int32) * MOE_BLOCK
    block_e = jnp.minimum(jnp.searchsorted(pends, blk_row0, side='right'),
                          N_EXPERTS - 1).astype(jnp.int32)
    n_used = (pends[-1] // MOE_BLOCK).astype(jnp.int32).reshape(1)
    return block_e, n_used, rows_tok, rows_dst, rows_gate.reshape(MOE_ROWS, 1)


def _moe_kernel(be_ref, nu_ref, tok_ref, dst_ref, fx_hbm, gate_ref, w1_ref, b1_ref, w2_ref,
                b2_ref, y_hbm, xg_scr, yb_scr, gsem, ssem):
    i = pl.program_id(0)
    n_used = nu_ref[0]
    slot = i % 2

    def gather_start(blk, sl, r):
        tok = tok_ref[blk * MOE_BLOCK + r]
        pltpu.make_async_copy(fx_hbm.at[pl.ds(tok, 1), :],
                              xg_scr.at[sl, pl.ds(r, 1), :], gsem.at[sl]).start()

    def scatter_start(blk, sl, r):
        dst = dst_ref[(blk + 1) * MOE_BLOCK + r]
        pltpu.make_async_copy(yb_scr.at[sl, pl.ds(r, 1), :],
                              y_hbm.at[pl.ds(dst, 1), :], ssem.at[sl]).start()

    def block_copy(sl, sem):
        return pltpu.make_async_copy(yb_scr.at[sl], y_hbm.at[pl.ds(N_SLOTS, MOE_BLOCK), :],
                                     sem.at[sl])

    def for_rows(fn):
        def body(r, c):
            fn(r)
            return c
        lax.fori_loop(0, MOE_BLOCK, body, 0)

    @pl.when(i == 0)
    def _():
        yb_scr[...] = jnp.zeros_like(yb_scr)
        block_copy(0, ssem).start()
        for_rows(lambda r: gather_start(0, 0, r))

    @pl.when(i < n_used)
    def _():
        block_copy(slot, gsem).wait()
        x = xg_scr[slot].astype(BF16)
        for r in range(MOE_BLOCK):
            gather_start(i + 1, 1 - slot, r)
        for r in range(MOE_BLOCK):
            scatter_start(i - 1, 1 - slot, r)
        hid = _dot(x, w1_ref[...]) + b1_ref[...]
        x_glu = jnp.minimum(hid[:, :D_FF], SWIGLU_LIMIT)
        x_lin = jnp.clip(hid[:, D_FF:], -SWIGLU_LIMIT, SWIGLU_LIMIT)
        act = x_glu * jax.nn.sigmoid(SWIGLU_ALPHA * x_glu) * (x_lin + 1.0)
        y = _dot(act.astype(BF16), w2_ref[...]) + b2_ref[...]
        block_copy(slot, ssem).wait()
        yb_scr[slot] = y * gate_ref[...]

    @pl.when(i == n_used)
    def _():
        block_copy(slot, gsem).wait()
        block_copy(slot, ssem).wait()
        for_rows(lambda r: scatter_start(i - 1, 1 - slot, r))
        block_copy(1 - slot, ssem).wait()


def _moe(fx, route, w1_b, b1, w2_b, b2):
    block_e, n_used, rows_tok, rows_dst, rows_gate = route
    grid_spec = pltpu.PrefetchScalarGridSpec(
        num_scalar_prefetch=4,
        grid=(MOE_NBLK,),
        in_specs=[
            pl.BlockSpec(memory_space=pl.ANY),
            pl.BlockSpec((MOE_BLOCK, 1), lambda i, be, *_: (i, 0)),
            pl.BlockSpec((None, D_MODEL, 2 * D_FF), lambda i, be, *_: (be[i], 0, 0)),
            pl.BlockSpec((None, 1, 2 * D_FF), lambda i, be, *_: (be[i], 0, 0)),
            pl.BlockSpec((None, D_FF, D_MODEL), lambda i, be, *_: (be[i], 0, 0)),
            pl.BlockSpec((None, 1, D_MODEL), lambda i, be, *_: (be[i], 0, 0)),
        ],
        out_specs=pl.BlockSpec(memory_space=pl.ANY),
        scratch_shapes=[pltpu.VMEM((2, MOE_BLOCK, D_MODEL), F32),
                        pltpu.VMEM((2, MOE_BLOCK, D_MODEL), F32),
                        pltpu.SemaphoreType.DMA((2,)),
                        pltpu.SemaphoreType.DMA((2,))],
    )
    return pl.pallas_call(
        _moe_kernel,
        out_shape=jax.ShapeDtypeStruct((MOE_OUT_ROWS, D_MODEL), F32),
        grid_spec=grid_spec,
        compiler_params=_params(), name="moe",
    )(block_e, n_used, rows_tok, rows_dst, fx, rows_gate, w1_b,
      b1.reshape(N_EXPERTS, 1, 2 * D_FF), w2_b, b2.reshape(N_EXPERTS, 1, D_MODEL))


def _combine_kernel(final, tile0, y4_ref, xs_ref, mod_ref, fn_ref, o_ref):
    is_ctx = _row_is_ctx(pl.program_id(0) + tile0, TILE)
    f = (y4_ref[:, 0:D_MODEL] + y4_ref[:, D_MODEL:2 * D_MODEL]
         + y4_ref[:, 2 * D_MODEL:3 * D_MODEL] + y4_ref[:, 3 * D_MODEL:])
    xs = xs_ref[...] + _mod_rows(mod_ref, 5, is_ctx) * f
    if final:
        xs = _rms_rows(xs) * fn_ref[...]
    o_ref[...] = xs


def _combine(y4, xs, mod_l, final_norm, final):
    y4 = y4.reshape(MOE_OUT_ROWS // TOP_K, TOP_K * D_MODEL)
    tile0 = CTX_LEN // TILE if final else 0
    n_out = SEQ if final else N_TOK
    return pl.pallas_call(
        functools.partial(_combine_kernel, final, tile0),
        out_shape=jax.ShapeDtypeStruct((n_out, D_MODEL), F32),
        grid=(n_out // TILE,),
        in_specs=[pl.BlockSpec((TILE, TOP_K * D_MODEL), lambda i: (i + tile0, 0)),
                  pl.BlockSpec((TILE, D_MODEL), lambda i: (i + tile0, 0)),
                  pl.BlockSpec((SUBLANES, 6 * D_MODEL), lambda i: (0, 0)),
                  pl.BlockSpec((1, D_MODEL), lambda i: (0, 0))],
        out_specs=pl.BlockSpec((TILE, D_MODEL), lambda i: (i, 0)),
        compiler_params=_params(), name="combine_final" if final else "combine",
    )(y4, xs, mod_l, final_norm.reshape(1, D_MODEL))


def _w_in_layout(w):
    g = GROUP_W
    s5, ret, lru, rw = w[:, :g], w[:, g:5 * g], w[:, 5 * g:7 * g], w[:, 7 * g:]
    z = jnp.zeros((D_MODEL, LANES - RWKV_RANK), w.dtype)
    rw = jnp.concatenate([rw[:, :3 * g], rw[:, 3 * g:3 * g + RWKV_RANK], z,
                          rw[:, 3 * g + RWKV_RANK:3 * g + 2 * RWKV_RANK], z,
                          rw[:, 3 * g + 2 * RWKV_RANK:]], axis=1)
    return jnp.concatenate([rw, s5, ret, lru], axis=1).astype(BF16)


def kernel(x, c, ctx, c_ctx, w_ada, b_ada, norm1, norm2, w_in, w_out, mix_gain, s5_lam_re, s5_lam_im, s5_log_dt, s5_b_re, s5_b_im, s5_c_re, s5_c_im, s5_d, s5_w_glu, s5_b_glu, ret_decay, lru_conv_w, lru_conv_b, lru_lam, lru_w_r, lru_b_r, lru_w_i, lru_b_i, rwkv_mu, rwkv_w0, rwkv_w_up, rwkv_a0, rwkv_a_up, rwkv_g_up, rwkv_k_k, rwkv_k_a, rwkv_r_k, rwkv_ln_w, rwkv_ln_b, router_w, router_b, exp_w1, exp_b1, exp_w2, exp_b2, final_norm):
    assert x.shape == (1, SEQ, D_MODEL) and ctx.shape == (1, CTX_LEN, D_MODEL)
    xs = jnp.concatenate([ctx[0], x[0]], axis=0).astype(F32)
    cc = jnp.zeros((SUBLANES, D_MODEL), F32).at[0].set(c[0]).at[1].set(c_ctx)
    mods = _modulation(cc, w_ada, b_ada)
    rope = _rope_tables()

    for l in range(DEPTH):
        mod_l = mods[l]
        proj = _inproj(xs, norm1[l], mod_l, _w_in_layout(w_in[l]))

        s5_tabs = [_s5_tables(s5_lam_re[l, d], s5_lam_im[l, d], s5_log_dt[l, d], s5_b_re[l, d],
                              s5_b_im[l, d], s5_c_re[l, d], s5_c_im[l, d], d == 1) for d in range(2)]
        ya = _s5_mixer(proj, s5_tabs[0], s5_tabs[1], s5_d[l], s5_w_glu[l], s5_b_glu[l])

        ret_f, ret_b = _ret_tables(ret_decay[l])
        yb = _ret_mixer(proj, rope, ret_f, ret_b)

        yc = _lru_mixer(proj, lru_conv_w[l], lru_conv_b[l], lru_lam[l], lru_w_r[l], lru_b_r[l],
                        lru_w_i[l], lru_b_i[l])

        r, v, kk, bonus, g, w, kd, b = _rwkv_prep(
            proj, rwkv_mu[l], rwkv_k_k[l], rwkv_k_a[l], rwkv_r_k[l], rwkv_w0[l], rwkv_w_up[l],
            rwkv_a0[l], rwkv_a_up[l], rwkv_g_up[l])
        ydf, ydb = _rwkv_scan(r, v, kk, w, kd, b)

        xs, fx, top_e, top_g = _merge(ya, yb, yc, ydf, ydb, bonus, g, rwkv_ln_w[l], rwkv_ln_b[l],
                                      mix_gain[l], w_out[l].astype(BF16), xs, mod_l, norm2[l],
                                      router_w[l], router_b[l])
        route = _route(top_e[:, :TOP_K], top_g[:, :TOP_K])
        y4 = _moe(fx, route, exp_w1[l].astype(BF16), exp_b1[l], exp_w2[l].astype(BF16), exp_b2[l])
        xs = _combine(y4, xs, mod_l, final_norm, l == DEPTH - 1)

    return xs.reshape(1, SEQ, D_MODEL)
```

```python
import functools
import math

import jax
import jax.numpy as jnp
from jax import lax
from jax.experimental import pallas as pl
from jax.experimental.pallas import tpu as pltpu

F32 = jnp.float32
BF16 = jnp.bfloat16

D_MODEL = 2048
SEQ = 8192
CTX_LEN = 256
N_TOK = SEQ + CTX_LEN
DEPTH = 4
GRID_W = 64
GROUP_W = 512
NORM_EPS = 1e-6
S5_CH = 16
S5_GROUPS = 32
S5_STATE = 64
RET_HEADS = 4
RET_DK = 128
RET_CHUNK = 128
ROPE_BASE = 10000.0
LRU_BLOCKS = 8
LRU_BW = 64
LRU_CONV = 4
LRU_C = 8.0
RWKV_HEAD = 64
RWKV_HEADS = 8
RWKV_RANK = 96
RWKV_GATE_RANK = 256
RWKV_LN_EPS = 64e-5
N_EXPERTS = 32
TOP_K = 4
D_FF = 896
SWIGLU_ALPHA = 1.702
SWIGLU_LIMIT = 7.0
MOE_BLOCK = 256

LANES = 128
SUBLANES = 8
VMEM_LIMIT = 56 * 1024 * 1024

N_PROJ = 5632
COL_RWKV = 0
COL_S5 = 2048
COL_RET = 2560
COL_LRU = 4608

TILE = 256
N_TILES = N_TOK // TILE
S5_SEG = 32
RW_TB = 64
RW_TILES = N_TOK // RW_TB
RW_CTX_TILES = CTX_LEN // RW_TB
RET_TILES = N_TOK // RET_CHUNK
RET_CTX_TILES = CTX_LEN // RET_CHUNK


def _params(n_axes=1):
    return pltpu.CompilerParams(
        dimension_semantics=("arbitrary",) * n_axes, vmem_limit_bytes=VMEM_LIMIT)


def _dot(a, b):
    return jnp.dot(a, b, preferred_element_type=F32)


def _split2(x):
    hi = x.astype(BF16)
    lo = (x - hi.astype(F32)).astype(BF16)
    return hi, lo


def _dot_x3(a, b):
    ah, al = _split2(a)
    bh, bl = _split2(b)
    return _dot(ah, bh) + _dot(ah, bl) + _dot(al, bh)


def _dot_sel(a, sel):
    ah, al = _split2(a)
    return _dot(ah, sel) + _dot(al, sel)


def _sel_dot(sel, a):
    ah, al = _split2(a)
    return _dot(sel, ah) + _dot(sel, al)


def _bwd_tile(s, n_ctx, n_all):
    return jnp.where(s < n_ctx, n_ctx - 1 - s, n_all - 1 - (s - n_ctx))


def _rms_rows(x):
    return x * lax.rsqrt(jnp.mean(x * x, axis=-1, keepdims=True) + NORM_EPS)


def _row_is_ctx(tile_idx, tile_rows):
    row = tile_idx * tile_rows + lax.broadcasted_iota(jnp.int32, (tile_rows, 1), 0)
    return row < CTX_LEN


def _mod_kernel(cc_ref, w_ref, b_ref, o_ref):
    s = cc_ref[...]
    s = s * jax.nn.sigmoid(s)
    o_ref[...] = _dot(s.astype(BF16), w_ref[...].astype(BF16)) + b_ref[...]


def _modulation(cc, w_ada, b_ada):
    tn = 1024
    nb = 6 * D_MODEL // tn
    return pl.pallas_call(
        _mod_kernel,
        out_shape=jax.ShapeDtypeStruct((DEPTH, SUBLANES, 6 * D_MODEL), F32),
        grid=(DEPTH, nb),
        in_specs=[
            pl.BlockSpec((SUBLANES, D_MODEL), lambda l, j: (0, 0)),
            pl.BlockSpec((None, D_MODEL, tn), lambda l, j: (l, 0, j)),
            pl.BlockSpec((None, 1, tn), lambda l, j: (l, 0, j)),
        ],
        out_specs=pl.BlockSpec((None, SUBLANES, tn), lambda l, j: (l, 0, j)),
        compiler_params=_params(2),
        name="modulation",
    )(cc, w_ada, b_ada.reshape(DEPTH, 1, 6 * D_MODEL))


def _mod_rows(mod_ref, chunk, is_ctx):
    lo, hi = chunk * D_MODEL, (chunk + 1) * D_MODEL
    return jnp.where(is_ctx, mod_ref[1:2, lo:hi], mod_ref[0:1, lo:hi])


IN_TM = 768
IN_TN = 512


def _inproj_kernel(x_ref, g_ref, mod_ref, w_ref, o_ref, h_scr):
    i, j = pl.program_id(0), pl.program_id(1)

    @pl.when(j == 0)
    def _():
        is_ctx = _row_is_ctx(i, IN_TM)
        xn = _rms_rows(x_ref[...]) * g_ref[...]
        h = xn * (1.0 + _mod_rows(mod_ref, 1, is_ctx)) + _mod_rows(mod_ref, 0, is_ctx)
        h_scr[...] = h.astype(BF16)

    o_ref[...] = _dot(h_scr[...], w_ref[...])


def _inproj(xs, norm_g, mod_l, w_in_b):
    return pl.pallas_call(
        _inproj_kernel,
        out_shape=jax.ShapeDtypeStruct((N_TOK, N_PROJ), F32),
        grid=(N_TOK // IN_TM, N_PROJ // IN_TN),
        in_specs=[
            pl.BlockSpec((IN_TM, D_MODEL), lambda i, j: (i, 0)),
            pl.BlockSpec((1, D_MODEL), lambda i, j: (0, 0)),
            pl.BlockSpec((SUBLANES, 6 * D_MODEL), lambda i, j: (0, 0)),
            pl.BlockSpec((D_MODEL, IN_TN), lambda i, j: (0, j)),
        ],
        out_specs=pl.BlockSpec((IN_TM, IN_TN), lambda i, j: (i, j)),
        scratch_shapes=[pltpu.VMEM((IN_TM, D_MODEL), BF16)],
        compiler_params=_params(2),
        name="inproj",
    )(xs, norm_g.reshape(1, D_MODEL), mod_l, w_in_b)


S5_NQ = 4
S5_QW = 2 * 8 * S5_STATE
S5_W = S5_NQ * S5_QW


def _s5_swap(h):
    half = S5_QW // 2
    parts = []
    for q in range(S5_NQ):
        parts.append(h[:, q * S5_QW + half:(q + 1) * S5_QW])
        parts.append(h[:, q * S5_QW:q * S5_QW + half])
    return jnp.concatenate(parts, axis=1)


def _s5_kernel(rev, *refs):
    if rev:
        (u_ref, yf_ref, perm_ref, permt_ref, bblk_ref, cblk_ref, a_ref, pw_ref,
         dskip_ref, wglu_ref, bglu_ref, o_ref, carry_scr, bu_scr, hl_scr, hs_scr) = refs
    else:
        (u_ref, perm_ref, permt_ref, bblk_ref, cblk_ref, a_ref, pw_ref,
         o_ref, carry_scr, bu_scr, hl_scr, hs_scr) = refs

    @pl.when(pl.program_id(0) == 0)
    def _():
        carry_scr[...] = jnp.zeros_like(carry_scr)

    u = u_ref[...]
    up = _dot(perm_ref[...], u.astype(BF16)).astype(BF16)
    for q in range(S5_NQ):
        bu = _dot(up[:, q * LANES:(q + 1) * LANES], bblk_ref[q])
        bu_scr[:, :, q * S5_QW:(q + 1) * S5_QW] = bu.reshape(S5_SEG, SUBLANES, S5_QW)

    a1, a2 = a_ref[0:1, :], a_ref[1:2, :]
    at1, at2 = a_ref[2:3, :], a_ref[3:4, :]

    def step(s, h):
        p = S5_SEG - 1 - s if rev else s
        h = a1 * h + a2 * _s5_swap(h) + bu_scr[p]
        hl_scr[p] = h
        return h

    ends = lax.fori_loop(0, S5_SEG, step, jnp.zeros((SUBLANES, S5_W), F32))

    c = carry_scr[...]
    for j in (range(SUBLANES - 1, -1, -1) if rev else range(SUBLANES)):
        hs_scr[j:j + 1, :] = c
        c = ends[j:j + 1, :] + at1 * c + at2 * _s5_swap(c)
    carry_scr[...] = c

    hs = hs_scr[...]
    hsw = _s5_swap(hs)

    def fix(p8, carry):
        base = pl.multiple_of(p8 * SUBLANES, SUBLANES)
        pw1 = pw_ref[0, pl.ds(base, SUBLANES), :]
        pw2 = pw_ref[1, pl.ds(base, SUBLANES), :]
        for tt in range(SUBLANES):
            p = base + tt
            hl_scr[p] = hl_scr[p] + pw1[tt:tt + 1, :] * hs + pw2[tt:tt + 1, :] * hsw
        return carry

    lax.fori_loop(0, S5_SEG // SUBLANES, fix, 0)

    ys = []
    for q in range(S5_NQ):
        hq = hl_scr[:, :, q * S5_QW:(q + 1) * S5_QW].reshape(TILE, S5_QW)
        ys.append(_dot(hq.astype(BF16), cblk_ref[q]))
    y = _sel_dot(permt_ref[...], jnp.concatenate(ys, axis=1))

    if rev:
        y = y + yf_ref[...] + dskip_ref[...] * u
        y = jax.nn.gelu(y)
        gate = _dot(y.astype(BF16), wglu_ref[...]) + bglu_ref[...]
        o_ref[...] = y * jax.nn.sigmoid(gate)
    else:
        o_ref[...] = y


def _s5_tables(lam_re, lam_im, log_dt, b_re, b_im, c_re, c_im, rev):
    dt = jnp.exp(log_dt.astype(F32))[:, None]
    lr = jnp.minimum(lam_re.astype(F32), -1e-4)
    li = lam_im.astype(F32)

    def a_pow(k):
        mag = jnp.exp(k * lr * dt)
        return mag * jnp.cos(k * li * dt), mag * jnp.sin(k * li * dt)

    def cols(re, im):
        lead = re.shape[:-2]
        re = re.reshape(lead + (S5_NQ, 8 * S5_STATE))
        im = im.reshape(lead + (S5_NQ, 8 * S5_STATE))
        return jnp.concatenate([re, im], axis=-1).reshape(lead + (S5_W,))

    ab_re, ab_im = a_pow(1.0)
    den = lr * lr + li * li
    f_re = ((ab_re - 1.0) * lr + ab_im * li) / den
    f_im = (ab_im * lr - (ab_re - 1.0) * li) / den
    bb_re = f_re[..., None] * b_re - f_im[..., None] * b_im
    bb_im = f_re[..., None] * b_im + f_im[..., None] * b_re
    at_re, at_im = a_pow(float(S5_SEG))
    a_tab = jnp.stack([cols(ab_re, ab_re), cols(-ab_im, ab_im),
                       cols(at_re, at_re), cols(-at_im, at_im)])
    ks = jnp.arange(S5_SEG, dtype=F32)
    ks = (S5_SEG - ks) if rev else (ks + 1.0)
    pk_re, pk_im = a_pow(ks[:, None, None])
    pw = jnp.stack([cols(pk_re, pk_re), cols(-pk_im, pk_im)])

    eye = jnp.eye(8, dtype=F32)

    def bdiag(m):
        a, b = m.shape[1:]
        m4 = m.reshape(S5_NQ, 8, a, b)
        return jnp.einsum('qgab,gh->qgahb', m4, eye).reshape(S5_NQ, 8 * a, 8 * b)

    bblk = jnp.concatenate([bdiag(jnp.swapaxes(bb_re, 1, 2)),
                            bdiag(jnp.swapaxes(bb_im, 1, 2))], axis=2).astype(BF16)
    cblk = jnp.concatenate([bdiag(jnp.swapaxes(c_re.astype(F32), 1, 2)),
                            bdiag(-jnp.swapaxes(c_im.astype(F32), 1, 2))], axis=1).astype(BF16)
    return a_tab, pw, bblk, cblk


def _s5_perm():
    r = jnp.arange(TILE)
    src = (r % SUBLANES) * S5_SEG + r // SUBLANES
    perm = (src[:, None] == jnp.arange(TILE)[None, :]).astype(BF16)
    return perm, perm.T


def _s5_mixer(proj, tabs_f, tabs_b, d_skip, w_glu, b_glu):
    perm, permt = _s5_perm()
    col = COL_S5 // GROUP_W

    def const(shape):
        return pl.BlockSpec(shape, lambda s: (0,) * len(shape))

    tab_specs = [const((TILE, TILE)), const((TILE, TILE)),
                 const((S5_NQ, LANES, S5_QW)), const((S5_NQ, S5_QW, LANES)),
                 const((4, S5_W)), const((2, S5_SEG, S5_W))]
    scratch = [pltpu.VMEM((1, S5_W), F32),
               pltpu.VMEM((S5_SEG, SUBLANES, S5_W), F32),
               pltpu.VMEM((S5_SEG, SUBLANES, S5_W), F32),
               pltpu.VMEM((SUBLANES, S5_W), F32)]
    out_shape = jax.ShapeDtypeStruct((N_TOK, GROUP_W), F32)

    a_tab, pw, bblk, cblk = tabs_f
    yf = pl.pallas_call(
        functools.partial(_s5_kernel, False),
        out_shape=out_shape, grid=(N_TILES,),
        in_specs=[pl.BlockSpec((TILE, GROUP_W), lambda s: (s, col))] + tab_specs,
        out_specs=pl.BlockSpec((TILE, GROUP_W), lambda s: (s, 0)),
        scratch_shapes=scratch, compiler_params=_params(), name="s5_fwd",
    )(proj, perm, permt, bblk, cblk, a_tab, pw)

    a_tab, pw, bblk, cblk = tabs_b
    bt = lambda s: _bwd_tile(s, 1, N_TILES)
    return pl.pallas_call(
        functools.partial(_s5_kernel, True),
        out_shape=out_shape, grid=(N_TILES,),
        in_specs=[pl.BlockSpec((TILE, GROUP_W), lambda s: (bt(s), col)),
                  pl.BlockSpec((TILE, GROUP_W), lambda s: (bt(s), 0))] + tab_specs
                 + [const((1, GROUP_W)), const((GROUP_W, GROUP_W)), const((1, GROUP_W))],
        out_specs=pl.BlockSpec((TILE, GROUP_W), lambda s: (bt(s), 0)),
        scratch_shapes=scratch, compiler_params=_params(), name="s5_bwd",
    )(proj, yf, perm, permt, bblk, cblk, a_tab, pw,
      d_skip.reshape(1, GROUP_W), w_glu.astype(BF16), b_glu.reshape(1, GROUP_W))


def _rope_shuffle(x):
    lane = lax.broadcasted_iota(jnp.int32, x.shape, 1)
    return jnp.where(lane % 64 < 32, pltpu.roll(x, LANES - 32, 1), pltpu.roll(x, 32, 1))


def _ret_kernel(rev, *refs):
    if rev:
        (q_ref, k_ref, v_ref, g_ref, of_ref, cos_ref, sin_ref, dintra_ref, dq_ref, dk_ref,
         dc_ref, o_ref, s_scr) = refs
    else:
        (q_ref, k_ref, v_ref, cos_ref, sin_ref, dintra_ref, dq_ref, dk_ref,
         dc_ref, o_ref, s_scr) = refs

    @pl.when(pl.program_id(0) == 0)
    def _():
        s_scr[...] = jnp.zeros_like(s_scr)

    cos, sin = cos_ref[...], sin_ref[...]
    outs = []
    for h in range(RET_HEADS):
        sl = slice(h * RET_DK, (h + 1) * RET_DK)
        qh = q_ref[:, sl]
        kh = k_ref[:, sl] * (RET_DK ** -0.5)
        qh = qh * cos + _rope_shuffle(qh) * sin
        kh = kh * cos + _rope_shuffle(kh) * sin
        qb, kb, vb = qh.astype(BF16), kh.astype(BF16), v_ref[:, sl].astype(BF16)
        sc = lax.dot_general(qb, kb, (((1,), (1,)), ((), ())),
                             preferred_element_type=F32) * dintra_ref[h]
        s_old = s_scr[h]
        o = _dot(sc.astype(BF16), vb) + _dot(qb, s_old.astype(BF16)) * dq_ref[h]
        kt = (kh * dk_ref[h]).T.astype(BF16)
        s_scr[h] = s_old * dc_ref[h] + _dot(kt, vb)
        if rev:
            o = o + of_ref[:, sl]
            gh = g_ref[:, sl]
            o = o * lax.rsqrt(jnp.mean(o * o, axis=-1, keepdims=True) + NORM_EPS)
            o = o * (gh * jax.nn.sigmoid(gh))
        outs.append(o)
    o_ref[...] = jnp.concatenate(outs, axis=1)


def _ret_tables(decay):
    log_g = -jnp.exp(decay.astype(F32))
    idx = jnp.arange(RET_CHUNK, dtype=F32)
    rel = idx[:, None] - idx[None, :]
    c = float(RET_CHUNK)
    lf = log_g[0][:, None, None]
    lb = log_g[1][:, None, None]
    d_f = jnp.where((rel >= 0)[None], jnp.exp(jnp.maximum(rel, 0.0)[None] * lf), 0.0)
    d_b = jnp.where((rel < 0)[None], jnp.exp(jnp.maximum(-rel, 0.0)[None] * lb), 0.0)
    ones = jnp.ones((1, 1, RET_DK), F32)
    dq_f = jnp.exp((idx + 1.0)[None, :, None] * lf) * ones
    dk_f = jnp.exp((c - 1.0 - idx)[None, :, None] * lf) * ones
    dq_b = jnp.exp((c - idx)[None, :, None] * lb) * ones
    dk_b = jnp.exp(idx[None, :, None] * lb) * ones
    dc_f = jnp.exp(c * lf) * ones
    dc_b = jnp.exp(c * lb) * ones
    return (d_f, dq_f, dk_f, dc_f), (d_b, dq_b, dk_b, dc_b)


def _rope_tables():
    m = 32
    inv = ROPE_BASE ** (-jnp.arange(m, dtype=F32) / m)
    t = jnp.arange(SEQ)
    ang_r = (t // GRID_W).astype(F32)[:, None] * inv[None, :]
    ang_c = (t % GRID_W).astype(F32)[:, None] * inv[None, :]
    cos = jnp.concatenate([jnp.cos(ang_r)] * 2 + [jnp.cos(ang_c)] * 2, axis=1)
    sin = jnp.concatenate([-jnp.sin(ang_r), jnp.sin(ang_r), -jnp.sin(ang_c), jnp.sin(ang_c)], axis=1)
    cos = jnp.concatenate([jnp.ones((CTX_LEN, RET_DK), F32), cos], axis=0)
    sin = jnp.concatenate([jnp.zeros((CTX_LEN, RET_DK), F32), sin], axis=0)
    return cos, sin


def _ret_mixer(proj, rope, tabs_f, tabs_b):
    cos, sin = rope
    c0 = COL_RET // GROUP_W

    def const(shape):
        return pl.BlockSpec(shape, lambda s: (0,) * len(shape))

    def tab_specs():
        return [const((RET_HEADS, RET_CHUNK, RET_CHUNK)), const((RET_HEADS, RET_CHUNK, RET_DK)),
                const((RET_HEADS, RET_CHUNK, RET_DK)), const((RET_HEADS, 1, RET_DK))]

    out_shape = jax.ShapeDtypeStruct((N_TOK, GROUP_W), F32)
    scratch = [pltpu.VMEM((RET_HEADS, RET_DK, RET_DK), F32)]

    def tok(cb, f):
        return pl.BlockSpec((RET_CHUNK, GROUP_W), lambda s: (f(s), cb))

    def rope_spec(f):
        return pl.BlockSpec((RET_CHUNK, RET_DK), lambda s: (f(s), 0))

    ident = lambda s: s
    of = pl.pallas_call(
        functools.partial(_ret_kernel, False),
        out_shape=out_shape, grid=(RET_TILES,),
        in_specs=[tok(c0, ident), tok(c0 + 1, ident), tok(c0 + 2, ident),
                  rope_spec(ident), rope_spec(ident)] + tab_specs(),
        out_specs=tok(0, ident),
        scratch_shapes=scratch, compiler_params=_params(), name="ret_fwd",
    )(proj, proj, proj, cos, sin, *tabs_f)

    bt = lambda s: _bwd_tile(s, RET_CTX_TILES, RET_TILES)
    return pl.pallas_call(
        functools.partial(_ret_kernel, True),
        out_shape=out_shape, grid=(RET_TILES,),
        in_specs=[tok(c0, bt), tok(c0 + 1, bt), tok(c0 + 2, bt), tok(c0 + 3, bt), tok(0, bt),
                  rope_spec(bt), rope_spec(bt)] + tab_specs(),
        out_specs=tok(0, bt),
        scratch_shapes=scratch, compiler_params=_params(), name="ret_bwd",
    )(proj, proj, proj, proj, of, cos, sin, *tabs_b)


def _lru_kernel(rev, *refs):
    if rev:
        (pv_ref, x_ref, nx_ref, gate_ref, hf_ref, cw_ref, cb_ref, sp_ref, wr_ref, br_ref,
         wi_ref, bi_ref, o_ref, carry_scr) = refs
    else:
        (pv_ref, x_ref, nx_ref, cw_ref, cb_ref, sp_ref, wr_ref, br_ref,
         wi_ref, bi_ref, o_ref, carry_scr) = refs
    s = pl.program_id(0)
    tile = _bwd_tile(s, 1, N_TILES) if rev else s

    @pl.when(s == 0)
    def _():
        carry_scr[...] = jnp.zeros_like(carry_scr)

    has_prev = jnp.logical_and(tile != 0, tile != 1).astype(F32)
    has_next = jnp.logical_and(tile != 0, tile != N_TILES - 1).astype(F32)
    x = x_ref[...]
    pv = pv_ref[...] * has_prev
    nx = nx_ref[...] * has_next
    row = lax.broadcasted_iota(jnp.int32, (TILE, GROUP_W), 0)
    xm1 = jnp.where(row == 0, pv[7:8, :], pltpu.roll(x, 1, 0))
    xm2 = jnp.where(row == 0, pv[6:7, :], jnp.where(row == 1, pv[7:8, :], pltpu.roll(x, 2, 0)))
    xp1 = jnp.where(row == TILE - 1, nx[0:1, :], pltpu.roll(x, TILE - 1, 0))
    xc = (xm2 * cw_ref[0:1, :] + xm1 * cw_ref[1:2, :] + x * cw_ref[2:3, :]
          + xp1 * cw_ref[3:4, :] + cb_ref[...])

    r = jax.nn.sigmoid(_dot_x3(xc, wr_ref[...]) + br_ref[...])
    ig = jax.nn.sigmoid(_dot_x3(xc, wi_ref[...]) + bi_ref[...])
    log_a = -LRU_C * r * sp_ref[...]
    a = jnp.exp(log_a)
    b = jnp.sqrt(1.0 - jnp.exp(2.0 * log_a)) * (ig * xc)

    sh = 1
    while sh < TILE:
        if rev:
            keep = row < TILE - sh
            a_s = jnp.where(keep, pltpu.roll(a, TILE - sh, 0), 1.0)
            b_s = jnp.where(keep, pltpu.roll(b, TILE - sh, 0), 0.0)
        else:
            keep = row >= sh
            a_s = jnp.where(keep, pltpu.roll(a, sh, 0), 1.0)
            b_s = jnp.where(keep, pltpu.roll(b, sh, 0), 0.0)
        b = a * b_s + b
        a = a * a_s
        sh *= 2
    h = b + a * carry_scr[...]
    carry_scr[...] = h[0:1, :] if rev else h[TILE - 1:TILE, :]
    if rev:
        o_ref[...] = (h + hf_ref[...]) * jax.nn.gelu(gate_ref[...])
    else:
        o_ref[...] = h


def _blockdiag(w):
    h, a, b = w.shape
    return jnp.einsum('hab,hg->hagb', w, jnp.eye(h, dtype=w.dtype)).reshape(h * a, h * b)


def _lru_mixer(proj, conv_w, conv_b, lam, w_r, b_r, w_i, b_i):
    cx = COL_LRU // GROUP_W
    rows8 = TILE // SUBLANES
    last8 = N_TOK // SUBLANES - 1

    def const(shape):
        return pl.BlockSpec(shape, lambda s: (0,) * len(shape))

    def specs(f, d):
        return dict(
            pv=pl.BlockSpec((SUBLANES, GROUP_W), lambda s: (jnp.maximum(f(s) * rows8 - 1, 0), cx)),
            x=pl.BlockSpec((TILE, GROUP_W), lambda s: (f(s), cx)),
            nx=pl.BlockSpec((SUBLANES, GROUP_W), lambda s: (jnp.minimum((f(s) + 1) * rows8, last8), cx)),
            gate=pl.BlockSpec((TILE, GROUP_W), lambda s: (f(s), cx + 1)),
            out=pl.BlockSpec((TILE, GROUP_W), lambda s: (f(s), 0)),
        )

    def dir_params(d):
        return (conv_w, conv_b.reshape(1, GROUP_W),
                jax.nn.softplus(-lam[d].astype(F32)).reshape(1, GROUP_W),
                _blockdiag(w_r[d]), b_r[d].reshape(1, GROUP_W),
                _blockdiag(w_i[d]), b_i[d].reshape(1, GROUP_W))

    par_specs = [const((LRU_CONV, GROUP_W)), const((1, GROUP_W)), const((1, GROUP_W)),
                 const((GROUP_W, GROUP_W)), const((1, GROUP_W)),
                 const((GROUP_W, GROUP_W)), const((1, GROUP_W))]
    out_shape = jax.ShapeDtypeStruct((N_TOK, GROUP_W), F32)
    scratch = [pltpu.VMEM((1, GROUP_W), F32)]

    sp = specs(lambda s: s, 0)
    hf = pl.pallas_call(
        functools.partial(_lru_kernel, False),
        out_shape=out_shape, grid=(N_TILES,),
        in_specs=[sp['pv'], sp['x'], sp['nx']] + par_specs,
        out_specs=sp['out'], scratch_shapes=scratch, compiler_params=_params(), name="lru_fwd",
    )(proj, proj, proj, *dir_params(0))

    sp = specs(lambda s: _bwd_tile(s, 1, N_TILES), 1)
    return pl.pallas_call(
        functools.partial(_lru_kernel, True),
        out_shape=out_shape, grid=(N_TILES,),
        in_specs=[sp['pv'], sp['x'], sp['nx'], sp['gate'], sp['out']] + par_specs,
        out_specs=sp['out'], scratch_shapes=scratch, compiler_params=_params(), name="lru_bwd",
    )(proj, proj, proj, proj, hf, *dir_params(1))


RW_IN = 2048
RW_HALO = 64


def _head_ones():
    h = jnp.arange(GROUP_W) // RWKV_HEAD
    return (h[:, None] == h[None, :]).astype(BF16)


def _rwkv_prep_kernel(pv_ref, cur_ref, nx_ref, mu_ref, kk_ref, ka_ref, rk_ref, w0_ref, wup_ref,
                      a0_ref, aup_ref, gup_ref, ones_ref,
                      r_o, v_o, kkn_o, bonus_o, g_o, w_o, kd_o, b_o, ext_scr, z_scr):
    i = pl.program_id(0)
    is_ctx = i == 0
    ext_scr[0:RW_HALO, :] = pv_ref[...]
    ext_scr[RW_HALO:RW_HALO + TILE, :] = cur_ref[...]
    ext_scr[RW_HALO + TILE:, :] = nx_ref[...]

    row = lax.broadcasted_iota(jnp.int32, (TILE, LANES), 0)
    lane = lax.broadcasted_iota(jnp.int32, (TILE, LANES), 1)
    c4 = lane % 4
    one = jnp.ones((TILE, LANES), F32)
    zero = jnp.zeros((TILE, LANES), F32)

    def mask(c):
        return jnp.where(c, one, zero)

    up_rows = jnp.where(i == 1, mask(row >= RW_HALO), one)
    dn_rows = jnp.where(i == N_TILES - 1, mask(row < TILE - RW_HALO), one)
    m_up = jnp.where(is_ctx, zero, up_rows * mask(c4 == 0))
    m_dn = jnp.where(is_ctx, zero, dn_rows * mask(c4 == 1))
    m_lt = jnp.where(is_ctx, mask(row >= 1) * mask(c4 % 2 == 0),
                     mask(row % GRID_W != 0) * mask(c4 == 2))
    m_rt = jnp.where(is_ctx, mask(row <= TILE - 2) * mask(c4 % 2 == 1),
                     mask(row % GRID_W != GRID_W - 1) * mask(c4 == 3))

    for cb in range(RW_IN // LANES):
        sl = slice(cb * LANES, (cb + 1) * LANES)
        p = ext_scr[RW_HALO:RW_HALO + TILE, sl]
        shifted = (ext_scr[0:TILE, sl] * m_up
                   + ext_scr[2 * RW_HALO:2 * RW_HALO + TILE, sl] * m_dn
                   + ext_scr[RW_HALO - 1:RW_HALO - 1 + TILE, sl] * m_lt
                   + ext_scr[RW_HALO + 1:RW_HALO + 1 + TILE, sl] * m_rt)
        z_scr[:, sl] = p + (shifted - p) * mu_ref[:, sl]

    gw = GROUP_W
    r = z_scr[:, 0:gw]
    k = z_scr[:, gw:2 * gw]
    v = z_scr[:, 2 * gw:3 * gw]
    wc = z_scr[:, 3 * gw:3 * gw + LANES]
    ac = z_scr[:, 3 * gw + LANES:3 * gw + 2 * LANES]
    gc = z_scr[:, 3 * gw + 2 * LANES:]
    ones = ones_ref[...]

    r_o[...] = r
    v_o[...] = v
    g_o[...] = _dot(jax.nn.sigmoid(gc).astype(BF16), gup_ref[...])
    kk = k * kk_ref[...]
    ss = _dot_sel(kk * kk, ones)
    kk = kk * lax.rsqrt(jnp.maximum(ss, 1e-12))
    kkn_o[...] = kk
    bonus_o[...] = _dot_sel(r * k * rk_ref[...], ones) * v
    tw = jnp.tanh(wc)
    for d in range(2):
        w_log = -jax.nn.softplus(-(w0_ref[d] + _dot_x3(tw, wup_ref[d]))) - 0.5
        w_o[d] = -jnp.exp(w_log)
        a = jax.nn.sigmoid(a0_ref[d] + _dot_x3(ac, aup_ref[d]))
        kd_o[d] = k * (1.0 + (a - 1.0) * ka_ref[...])
        b_o[d] = kk * a


def _rwkv_prep(proj, mu, k_k, k_a, r_k, w0, w_up, a0, a_up, g_up):
    blk64 = TILE // RW_HALO
    last64 = N_TOK // RW_HALO - 1

    def const(shape):
        return pl.BlockSpec(shape, lambda i: (0,) * len(shape))

    tok = pl.BlockSpec((TILE, GROUP_W), lambda i: (i, 0))
    tok2 = pl.BlockSpec((2, TILE, GROUP_W), lambda i: (0, i, 0))
    one = jax.ShapeDtypeStruct((N_TOK, GROUP_W), F32)
    two = jax.ShapeDtypeStruct((2, N_TOK, GROUP_W), F32)
    pad = LANES - RWKV_RANK

    def pad_rows(w):
        return jnp.pad(w.astype(F32), ((0, 0), (0, pad), (0, 0)))

    def pad_mu(m):
        z = jnp.zeros((pad,), F32)
        g3 = 3 * GROUP_W
        return jnp.concatenate([m[:g3], m[g3:g3 + RWKV_RANK], z,
                                m[g3 + RWKV_RANK:g3 + 2 * RWKV_RANK], z,
                                m[g3 + 2 * RWKV_RANK:]]).reshape(1, RW_IN)

    return pl.pallas_call(
        _rwkv_prep_kernel,
        out_shape=(one, one, one, one, one, two, two, two),
        grid=(N_TILES,),
        in_specs=[
            pl.BlockSpec((RW_HALO, RW_IN), lambda i: (jnp.maximum(i * blk64 - 1, 0), 0)),
            pl.BlockSpec((TILE, RW_IN), lambda i: (i, 0)),
            pl.BlockSpec((RW_HALO, RW_IN), lambda i: (jnp.minimum((i + 1) * blk64, last64), 0)),
            const((1, RW_IN)), const((1, GROUP_W)), const((1, GROUP_W)), const((1, GROUP_W)),
            const((2, 1, GROUP_W)), const((2, LANES, GROUP_W)),
            const((2, 1, GROUP_W)), const((2, LANES, GROUP_W)),
            const((RWKV_GATE_RANK, GROUP_W)), const((GROUP_W, GROUP_W)),
        ],
        out_specs=(tok, tok, tok, tok, tok, tok2, tok2, tok2),
        scratch_shapes=[pltpu.VMEM((TILE + 2 * RW_HALO, RW_IN), F32),
                        pltpu.VMEM((TILE, RW_IN), F32)],
        compiler_params=_params(), name="rwkv_prep",
    )(proj, proj, proj, pad_mu(mu.astype(F32)), k_k.reshape(1, GROUP_W), k_a.reshape(1, GROUP_W),
      r_k.reshape(1, GROUP_W), w0.reshape(2, 1, GROUP_W), pad_rows(w_up),
      a0.reshape(2, 1, GROUP_W), pad_rows(a_up), g_up.astype(BF16), _head_ones())


def _rwkv_scan_kernel(rf, vf, kf, wf, kdf, bf, rb, vb, kb, wb, kdb, bb, eye_ref, bd_ref,
                      yf_o, yb_o, s_scr, vexp_scr, yr_scr):
    @pl.when(pl.program_id(0) == 0)
    def _():
        s_scr[...] = jnp.zeros_like(s_scr)

    n_pair = GROUP_W // LANES
    dirs = ((rf, vf, kf, wf, kdf, bf), (rb, vb, kb, wb, kdb, bb))
    eye_b = eye_ref[...].astype(BF16)
    bd = bd_ref[...]

    for d in range(2):
        v_ref = dirs[d][1]
        for hq in range(GROUP_W // RW_BW):
            sl = slice(hq * RW_BW, (hq + 1) * RW_BW)
            vp = v_ref[:, sl].astype(BF16)
            lhs = (vp[:, None, :] * eye_b[None]).reshape(RW_TB * RWKV_HEAD, RW_BW)
            vexp_scr[d, :, :, sl] = _dot(lhs, bd).reshape(RW_TB, RWKV_HEAD, RW_BW)

    lane = lax.broadcasted_iota(jnp.int32, (RWKV_HEAD, LANES), 1)
    low = lane < RWKV_HEAD

    n_grp = RW_TB // SUBLANES

    def step(sg, carry):
        bases = (pl.multiple_of(sg * SUBLANES, SUBLANES),
                 pl.multiple_of((n_grp - 1 - sg) * SUBLANES, SUBLANES))
        rows = [[[ref[pl.ds(bases[d], SUBLANES), pr * LANES:(pr + 1) * LANES]
                  for ref in (dirs[d][0],) + dirs[d][2:]]
                 for pr in range(n_pair)] for d in range(2)]
        for tt in range(SUBLANES):
            for d in range(2):
                ti = tt if d == 0 else SUBLANES - 1 - tt
                t = bases[d] + ti
                for pr in range(n_pair):
                    sl = slice(pr * LANES, (pr + 1) * LANES)
                    r_t, kk_t, w_t, kd_t, b_t = [a[ti:ti + 1, :] for a in rows[d][pr]]
                    st = s_scr[d, :, sl]
                    if d == 1:
                        yr_scr[d, t, :, sl] = (st * r_t).astype(BF16)
                    x = st * kk_t
                    tot = jnp.sum(x, axis=1, keepdims=True)
                    lo = jnp.sum(jnp.where(low, x, 0.0), axis=1, keepdims=True)
                    sa = jnp.where(low, lo, tot - lo)
                    st = st * w_t - sa * b_t + vexp_scr[d, t, :, sl] * kd_t
                    s_scr[d, :, sl] = st
                    if d == 0:
                        yr_scr[d, t, :, sl] = (st * r_t).astype(BF16)
        return carry

    lax.fori_loop(0, n_grp, step, 0)

    eye_f = eye_ref[...]
    for d, y_o in enumerate((yf_o, yb_o)):
        for hq in range(GROUP_W // RW_BW):
            sl = slice(hq * RW_BW, (hq + 1) * RW_BW)
            ys = _dot(yr_scr[d, :, :, sl].reshape(RW_TB * RWKV_HEAD, RW_BW), bd)
            y_o[:, sl] = jnp.sum(ys.reshape(RW_TB, RWKV_HEAD, RW_BW) * eye_f[None], axis=1)


RW_C = 64
RW_T2 = 256
RW_NP = 2


def _split3(x):
    h1 = x.astype(BF16)
    h2 = (x - h1.astype(F32)).astype(BF16)
    h3 = (x - h1.astype(F32) - h2.astype(F32)).astype(BF16)
    return h1, h2, h3


def _rwkv_chunk_kernel(rf, vf, kf, lwf, kdf, bf, rb, vb, kb, lwb, kdb, bb, yf_o, yb_o, st_scr):
    @pl.when(pl.program_id(1) == 0)
    def _():
        st_scr[...] = jnp.zeros_like(st_scr)

    c = RW_C
    n_ch = RW_T2 // c
    ti = lax.broadcasted_iota(jnp.int32, (c, c), 0)
    si = lax.broadcasted_iota(jnp.int32, (c, c), 1)
    lane = lax.broadcasted_iota(jnp.int32, (1, LANES), 1)
    hmask = [jnp.where(lane < RWKV_HEAD, 1.0, 0.0), jnp.where(lane >= RWKV_HEAD, 1.0, 0.0)]
    bi = lax.broadcasted_iota(jnp.int32, (LANES, LANES), 0) // RWKV_HEAD
    bj = lax.broadcasted_iota(jnp.int32, (LANES, LANES), 1) // RWKV_HEAD
    bdmask = jnp.where(bi == bj, 1.0, 0.0)
    eye = jnp.where(ti == si, 1.0, 0.0)

    def nt(a, b):
        return lax.dot_general(a, b, (((1,), (1,)), ((), ())), preferred_element_type=F32)

    def mm1(a, b):
        return _dot(a.astype(BF16), b.astype(BF16))

    mm_inv = mm1
    mm_aux = mm1
    mm_st = mm1

    refs = ((rf, vf, kf, lwf, kdf, bf), (rb, vb, kb, lwb, kdb, bb))
    units = [(d, (ci if d == 0 else n_ch - 1 - ci) * c, pr)
             for ci in range(n_ch) for d in range(2) for pr in range(RW_NP)]

    pre = {}
    for d, lo, pr in units:
        r_ref, v_ref, kk_ref, lw_ref, kd_ref, b_ref = refs[d]
        rows = slice(lo, lo + c)
        cols = slice(pr * LANES, (pr + 1) * LANES)
        strict = (si < ti) if d == 0 else (si > ti)
        incl = (si <= ti) if d == 0 else (si >= ti)
        ymask = incl if d == 0 else strict
        lw = lw_ref[rows, cols]
        tri = jnp.where(incl, 1.0, 0.0).astype(BF16)
        l1, l2, l3 = _split3(lw)
        logpi = _dot(tri, l1) + _dot(tri, l2) + _dot(tri, l3)
        pe = jnp.exp(logpi - lw)
        inv = jnp.exp(-logpi)
        qt = kk_ref[rows, cols] * pe
        kt = kd_ref[rows, cols] * inv
        bt = b_ref[rows, cols] * inv
        rt = r_ref[rows, cols] * (jnp.exp(logpi) if d == 0 else pe)
        v = v_ref[rows, cols]
        ktb, btb = kt.astype(BF16), bt.astype(BF16)
        heads = []
        for h in range(2):
            qh = (qt * hmask[h]).astype(BF16)
            rh = (rt * hmask[h]).astype(BF16)
            heads.append(dict(
                ak=jnp.where(strict, nt(qh, ktb), 0.0), ab=jnp.where(strict, nt(qh, btb), 0.0),
                mk=jnp.where(ymask, nt(rh, ktb), 0.0), mb=jnp.where(ymask, nt(rh, btb), 0.0),
                vh=v * hmask[h]))
        pre[(d, lo, pr)] = dict(qt=qt, rt=rt, heads=heads, ktv=mm1(kt.T, v), btt=bt.T,
                            ptot=jnp.exp(jnp.sum(lw.T, axis=1, keepdims=True)))

    inst = [hd for key in units for hd in pre[key]['heads']]
    for hd in inst:
        hd['pw'] = hd['ab']
        hd['x'] = eye - hd['ab']
    for _ in range(5):
        for hd in inst:
            hd['pw'] = mm_inv(hd['pw'], hd['pw'])
        for hd in inst:
            hd['x'] = hd['x'] + mm_inv(hd['x'], hd['pw'])
    eye_k = jnp.where(lax.broadcasted_iota(jnp.int32, (LANES, LANES), 0)
                      == lax.broadcasted_iota(jnp.int32, (LANES, LANES), 1), 1.0, 0.0)
    for key in units:
        pc = pre[key]
        wtq, ypre = None, None
        for h, hd in enumerate(pc['heads']):
            rhs = jnp.concatenate([mm_aux(hd['ak'], hd['vh']), pc['qt'] * hmask[h]], axis=1)
            t_rhs = mm_aux(hd['x'], rhs)
            yh = jnp.concatenate([mm_aux(hd['mk'], hd['vh']), pc['rt'] * hmask[h]], axis=1) \
                - mm_aux(hd['mb'], t_rhs)
            wtq = t_rhs if wtq is None else wtq + t_rhs
            ypre = yh if ypre is None else ypre + yh
        bw = mm_aux(pc['btt'], wtq)
        pc['ypre'] = ypre
        pc['cst'] = bdmask * (pc['ktv'] - bw[:, :LANES])
        pc['mtx'] = eye_k - bdmask * bw[:, LANES:]

    state = {(d, pr): st_scr[d, pr] for d in range(2) for pr in range(RW_NP)}
    for d, lo, pr in units:
        pc = pre[(d, lo, pr)]
        s0 = state[(d, pr)]
        y = pc['ypre'][:, :LANES] + mm_st(pc['ypre'][:, LANES:], s0)
        (yf_o if d == 0 else yb_o)[lo:lo + c, pr * LANES:(pr + 1) * LANES] = y
        state[(d, pr)] = pc['ptot'] * (_dot_x3(pc['mtx'], s0) + pc['cst'])
    for (d, pr), val in state.items():
        st_scr[d, pr] = val


def _rwkv_chunk_kernel_old(rf, vf, kf, lwf, kdf, bf, rb, vb, kb, lwb, kdb, bb, yf_o, yb_o, st_scr):
    @pl.when(pl.program_id(1) == 0)
    def _():
        st_scr[...] = jnp.zeros_like(st_scr)

    c = RW_C
    ti = lax.broadcasted_iota(jnp.int32, (c, c), 0)
    si = lax.broadcasted_iota(jnp.int32, (c, c), 1)
    lane = lax.broadcasted_iota(jnp.int32, (1, LANES), 1)
    hmask = [jnp.where(lane < RWKV_HEAD, 1.0, 0.0), jnp.where(lane >= RWKV_HEAD, 1.0, 0.0)]
    bi = lax.broadcasted_iota(jnp.int32, (LANES, LANES), 0) // RWKV_HEAD
    bj = lax.broadcasted_iota(jnp.int32, (LANES, LANES), 1) // RWKV_HEAD
    bdmask = jnp.where(bi == bj, 1.0, 0.0)
    eye = jnp.where(ti == si, 1.0, 0.0)

    def nt(a, b):
        return lax.dot_general(a, b, (((1,), (1,)), ((), ())), preferred_element_type=F32)

    def mm1(a, b):
        return _dot(a.astype(BF16), b.astype(BF16))

    def chunk(d, refs, lo, y_o):
        r_ref, v_ref, kk_ref, lw_ref, kd_ref, b_ref = refs
        rows = slice(lo, lo + c)
        strict = (si < ti) if d == 0 else (si > ti)
        incl = (si <= ti) if d == 0 else (si >= ti)
        lw = lw_ref[rows, :]
        r, v, kk, kd, b = r_ref[rows, :], v_ref[rows, :], kk_ref[rows, :], kd_ref[rows, :], b_ref[rows, :]

        tri = jnp.where(incl, 1.0, 0.0).astype(BF16)
        l1 = lw.astype(BF16)
        l2 = (lw - l1.astype(F32)).astype(BF16)
        l3 = (lw - l1.astype(F32) - l2.astype(F32)).astype(BF16)
        logpi = _dot(tri, l1) + _dot(tri, l2) + _dot(tri, l3)
        logpe = logpi - lw
        pe = jnp.exp(logpe)
        inv = jnp.exp(-logpi)
        qt = kk * pe
        kt = kd * inv
        bt = b * inv
        rt = r * (jnp.exp(logpi) if d == 0 else pe)
        tot = jnp.sum(lw.T, axis=1, keepdims=True)

        s0 = st_scr[d]
        ymask = incl if d == 0 else strict
        rhs = _dot_x3(qt, s0)
        y = _dot_x3(rt, s0)
        qb, ktb, btb, rb_ = qt.astype(BF16), kt.astype(BF16), bt.astype(BF16), rt.astype(BF16)
        u = jnp.zeros((c, LANES), F32)
        for h in range(2):
            m = hmask[h]
            qh = (qt * m).astype(BF16)
            rh = (rt * m).astype(BF16)
            ak = jnp.where(strict, nt(qh, ktb), 0.0)
            ab = jnp.where(strict, nt(qh, btb), 0.0)
            mk = jnp.where(ymask, nt(rh, ktb), 0.0)
            mb = jnp.where(ymask, nt(rh, btb), 0.0)
            vh = v * m
            g = (rhs * m) + _dot_x3(ak, vh)
            a2 = _dot_x3(ab, ab)
            a4 = _dot_x3(a2, a2)
            a8 = _dot_x3(a4, a4)
            a16 = _dot_x3(a8, a8)
            a32 = _dot_x3(a16, a16)
            for ap in (a32, a16, a8, a4, a2):
                g = g + _dot_x3(ap, g)
            uh = g - _dot_x3(ab, g)
            u = u + uh
            y = y + _dot_x3(mk, vh) - _dot_x3(mb, uh)
        y_o[rows, :] = y
        upd = mm1(kt.T, v) - mm1(bt.T, u)
        st_scr[d] = jnp.exp(tot) * (s0 + bdmask * upd)

    n_ch = RW_T2 // c
    fwd = (rf, vf, kf, lwf, kdf, bf)
    bwd = (rb, vb, kb, lwb, kdb, bb)
    for ci in range(n_ch):
        chunk(0, fwd, ci * c, yf_o)
        chunk(1, bwd, (n_ch - 1 - ci) * c, yb_o)


def _rwkv_chunks(r, v, kk, lw, kd, b):
    n_t = N_TOK // RW_T2
    ft = lambda s: s
    bt = lambda s: _bwd_tile(s, CTX_LEN // RW_T2, n_t)

    bw = RW_NP * LANES

    def one(f):
        return pl.BlockSpec((RW_T2, bw), lambda p, s: (f(s), p))

    def two(f, d):
        return pl.BlockSpec((None, RW_T2, bw), lambda p, s: (d, f(s), p))

    out = jax.ShapeDtypeStruct((N_TOK, GROUP_W), F32)
    return pl.pallas_call(
        _rwkv_chunk_kernel,
        out_shape=(out, out),
        grid=(GROUP_W // bw, n_t),
        in_specs=[one(ft), one(ft), one(ft), two(ft, 0), two(ft, 0), two(ft, 0),
                  one(bt), one(bt), one(bt), two(bt, 1), two(bt, 1), two(bt, 1)],
        out_specs=(one(ft), one(bt)),
        scratch_shapes=[pltpu.VMEM((2, RW_NP, LANES, LANES), F32)],
        compiler_params=_params(2), name="rwkv_chunks",
    )(r, v, kk, lw, kd, b, r, v, kk, lw, kd, b)


RW_BW = 256


def _rwkv_scan(r, v, kk, w, kd, b):
    i64 = jnp.arange(RWKV_HEAD)
    lw = jnp.arange(RW_BW)
    eye2 = (i64[:, None] == (lw % RWKV_HEAD)[None, :]).astype(F32)
    bd = ((lw // RWKV_HEAD)[:, None] == (lw // RWKV_HEAD)[None, :]).astype(BF16)

    ft = lambda s: s
    bt = lambda s: _bwd_tile(s, RW_CTX_TILES, RW_TILES)

    def one(f):
        return pl.BlockSpec((RW_TB, GROUP_W), lambda s: (f(s), 0))

    def two(f, d):
        return pl.BlockSpec((None, RW_TB, GROUP_W), lambda s: (d, f(s), 0))

    out = jax.ShapeDtypeStruct((N_TOK, GROUP_W), F32)
    return pl.pallas_call(
        _rwkv_scan_kernel,
        out_shape=(out, out),
        grid=(RW_TILES,),
        in_specs=[one(ft), one(ft), one(ft), two(ft, 0), two(ft, 0), two(ft, 0),
                  one(bt), one(bt), one(bt), two(bt, 1), two(bt, 1), two(bt, 1),
                  pl.BlockSpec((RWKV_HEAD, RW_BW), lambda s: (0, 0)),
                  pl.BlockSpec((RW_BW, RW_BW), lambda s: (0, 0))],
        out_specs=(one(ft), one(bt)),
        scratch_shapes=[pltpu.VMEM((2, RWKV_HEAD, GROUP_W), F32),
                        pltpu.VMEM((2, RW_TB, RWKV_HEAD, GROUP_W), F32),
                        pltpu.VMEM((2, RW_TB, RWKV_HEAD, GROUP_W), BF16)],
        compiler_params=_params(), name="rwkv_scan",
    )(r, v, kk, w, kd, b, r, v, kk, w, kd, b, eye2, bd)


def _merge_kernel(ya_ref, yb_ref, yc_ref, ydf_ref, ydb_ref, bonus_ref, g_ref, lnw_ref, lnb_ref,
                  ones_ref, gain_ref, wout_ref, xs_ref, mod_ref, n2_ref, rw_ref, rb_ref,
                  xo_ref, fx_ref, te_ref, tg_ref):
    i = pl.program_id(0)
    is_ctx = _row_is_ctx(i, TILE)
    ones = ones_ref[...]
    inv = 1.0 / RWKV_HEAD
    yd = ydf_ref[...] + ydb_ref[...]
    mean = _dot_sel(yd, ones) * inv
    dl = yd - mean
    var = _dot_sel(dl * dl, ones) * inv
    yd = (dl * lax.rsqrt(var + RWKV_LN_EPS) * lnw_ref[...] + lnb_ref[...] + bonus_ref[...]) * g_ref[...]

    parts = []
    for gi, y in enumerate((ya_ref[...], yb_ref[...], yc_ref[...], yd)):
        parts.append((_rms_rows(y) * gain_ref[:, gi * GROUP_W:(gi + 1) * GROUP_W]).astype(BF16))
    m = _dot(jnp.concatenate(parts, axis=1), wout_ref[...])
    xs = xs_ref[...] + _mod_rows(mod_ref, 2, is_ctx) * m
    xo_ref[...] = xs

    fx = _rms_rows(xs) * n2_ref[...]
    fx = fx * (1.0 + _mod_rows(mod_ref, 4, is_ctx)) + _mod_rows(mod_ref, 3, is_ctx)
    fx_ref[...] = fx

    logits = _dot_x3(fx, rw_ref[...]) + rb_ref[...]
    lane = lax.broadcasted_iota(jnp.int32, logits.shape, 1)
    vals = logits
    tops, idxs = [], []
    for _ in range(TOP_K):
        mx = jnp.max(vals, axis=-1, keepdims=True)
        ix = jnp.min(jnp.where(vals == mx, lane, LANES), axis=-1, keepdims=True)
        tops.append(mx)
        idxs.append(ix)
        vals = jnp.where(lane == ix, -jnp.inf, vals)
    es = [jnp.exp(t - tops[0]) for t in tops]
    den = es[0] + es[1] + es[2] + es[3]
    te = jnp.zeros(logits.shape, jnp.int32)
    tg = jnp.zeros(logits.shape, F32)
    for kx in range(TOP_K):
        te = jnp.where(lane == kx, idxs[kx], te)
        tg = jnp.where(lane == kx, es[kx] / den, tg)
    te_ref[...] = te
    tg_ref[...] = tg


def _merge(ya, yb, yc, ydf, ydb, bonus, g, ln_w, ln_b, gain, w_out_b, xs, mod_l, norm2,
           router_w, router_b):
    def const(shape):
        return pl.BlockSpec(shape, lambda i: (0,) * len(shape))

    grp = pl.BlockSpec((TILE, GROUP_W), lambda i: (i, 0))
    full = pl.BlockSpec((TILE, D_MODEL), lambda i: (i, 0))
    lanes = pl.BlockSpec((TILE, LANES), lambda i: (i, 0))
    rw = jnp.pad(router_w.astype(F32), ((0, 0), (0, LANES - N_EXPERTS)))
    rb = jnp.concatenate([router_b.astype(F32), jnp.full((LANES - N_EXPERTS,), -1e30, F32)])
    return pl.pallas_call(
        _merge_kernel,
        out_shape=(jax.ShapeDtypeStruct((N_TOK, D_MODEL), F32),
                   jax.ShapeDtypeStruct((N_TOK, D_MODEL), F32),
                   jax.ShapeDtypeStruct((N_TOK, LANES), jnp.int32),
                   jax.ShapeDtypeStruct((N_TOK, LANES), F32)),
        grid=(N_TILES,),
        in_specs=[grp] * 7 + [const((1, GROUP_W)), const((1, GROUP_W)), const((GROUP_W, GROUP_W)),
                              const((1, D_MODEL)), const((D_MODEL, D_MODEL)), full,
                              const((SUBLANES, 6 * D_MODEL)), const((1, D_MODEL)),
                              const((D_MODEL, LANES)), const((1, LANES))],
        out_specs=(full, full, lanes, lanes),
        compiler_params=_params(), name="merge",
    )(ya, yb, yc, ydf, ydb, bonus, g, ln_w.reshape(1, GROUP_W), ln_b.reshape(1, GROUP_W),
      _head_ones(), gain.reshape(1, D_MODEL), w_out_b, xs, mod_l, norm2.reshape(1, D_MODEL),
      rw, rb.reshape(1, LANES))


N_SLOTS = N_TOK * TOP_K
MOE_NBLK = N_SLOTS // MOE_BLOCK + N_EXPERTS
MOE_ROWS = MOE_NBLK * MOE_BLOCK
MOE_OUT_ROWS = N_SLOTS + 2 * MOE_BLOCK


ROUTE_PARTS = 2


def _route_kernel(part, e_ref, cur0_ref, tok0_hbm, dst0_hbm, tok_ref, dst_ref, cur_ref, sem):
    copies = [pltpu.make_async_copy(tok0_hbm, tok_ref, sem.at[0]),
              pltpu.make_async_copy(dst0_hbm, dst_ref, sem.at[1])]
    for cp in copies:
        cp.start()
    for cp in copies:
        cp.wait()
    for e in range(N_EXPERTS):
        cur_ref[e] = cur0_ref[e]

    def place(t, c):
        es = [e_ref[t * TOP_K + k] for k in range(TOP_K)]
        ps = [cur_ref[e] for e in es]
        for k in range(TOP_K):
            cur_ref[es[k]] = ps[k] + 1
        for k in range(TOP_K):
            tok_ref[ps[k]] = t
            dst_ref[ps[k] + MOE_BLOCK] = k * N_TOK + t
        return c
    n = N_TOK // ROUTE_PARTS
    lax.fori_loop(part * n, (part + 1) * n, place, 0)


def _route(top_e):
    assert MOE_BLOCK == 256 and TOP_K == 4
    flat_e = top_e.reshape(N_SLOTS)
    counts = jnp.sum((flat_e[:, None] == jnp.arange(N_EXPERTS, dtype=jnp.int32)[None, :])
                     .astype(jnp.int32), axis=0)
    padded = (counts + MOE_BLOCK - 1) // MOE_BLOCK * MOE_BLOCK
    pends = jnp.cumsum(padded)
    smem = pl.BlockSpec(memory_space=pltpu.SMEM)
    row = jnp.arange(MOE_ROWS + MOE_BLOCK, dtype=jnp.int32) - MOE_BLOCK
    dump = N_SLOTS + ((row // MOE_BLOCK) % 2) * MOE_BLOCK + row % MOE_BLOCK
    cur = (pends - padded).astype(jnp.int32)
    rows_tok, rows_dst = jnp.zeros((MOE_ROWS,), jnp.int32), dump
    for part in range(ROUTE_PARTS):
        rows_tok, rows_dst, cur = pl.pallas_call(
            functools.partial(_route_kernel, part),
            out_shape=(jax.ShapeDtypeStruct((MOE_ROWS,), jnp.int32),
                       jax.ShapeDtypeStruct((MOE_ROWS + MOE_BLOCK,), jnp.int32),
                       jax.ShapeDtypeStruct((N_EXPERTS,), jnp.int32)),
            in_specs=[smem, smem, pl.BlockSpec(memory_space=pl.ANY), pl.BlockSpec(memory_space=pl.ANY)],
            out_specs=(smem, smem, smem),
            scratch_shapes=[pltpu.SemaphoreType.DMA((2,))],
            name="route",
        )(flat_e, cur, rows_tok, rows_dst)
    blk_row0 = jnp.arange(MOE_NBLK, dtype=jnp.int32) * MOE_BLOCK
    block_e = jnp.minimum(jnp.searchsorted(pends, blk_row0, side='right'),
                          N_EXPERTS - 1).astype(jnp.int32)
    n_used = (pends[-1] // MOE_BLOCK).astype(jnp.int32).reshape(1)
    return block_e, n_used, rows_tok, rows_dst


def _moe_kernel(be_ref, nu_ref, tok_ref, dst_ref, fx_hbm, w1_ref, b1_ref, w2_ref,
                b2_ref, y_hbm, xg_scr, yb_scr, gsem, ssem):
    i = pl.program_id(0)
    n_used = nu_ref[0]
    slot = i % 2

    def gather_start(blk, sl, r):
        tok = tok_ref[blk * MOE_BLOCK + r]
        pltpu.make_async_copy(fx_hbm.at[pl.ds(tok, 1), :],
                              xg_scr.at[sl, pl.ds(r, 1), :], gsem.at[sl]).start()

    def scatter_start(blk, sl, r):
        dst = dst_ref[(blk + 1) * MOE_BLOCK + r]
        pltpu.make_async_copy(yb_scr.at[sl, pl.ds(r, 1), :],
                              y_hbm.at[pl.ds(dst, 1), :], ssem.at[sl]).start()

    def block_copy(sl, sem):
        return pltpu.make_async_copy(yb_scr.at[sl], y_hbm.at[pl.ds(N_SLOTS, MOE_BLOCK), :],
                                     sem.at[sl])

    def for_rows(fn):
        def body(r, c):
            fn(r)
            return c
        lax.fori_loop(0, MOE_BLOCK, body, 0)

    @pl.when(i == 0)
    def _():
        yb_scr[...] = jnp.zeros_like(yb_scr)
        block_copy(0, ssem).start()
        for_rows(lambda r: gather_start(0, 0, r))

    @pl.when(i < n_used)
    def _():
        block_copy(slot, gsem).wait()
        x = xg_scr[slot].astype(BF16)
        for r in range(MOE_BLOCK):
            gather_start(i + 1, 1 - slot, r)
        for r in range(MOE_BLOCK):
            scatter_start(i - 1, 1 - slot, r)
        hid = _dot(x, w1_ref[...]) + b1_ref[...]
        x_glu = jnp.minimum(hid[:, :D_FF], SWIGLU_LIMIT)
        x_lin = jnp.clip(hid[:, D_FF:], -SWIGLU_LIMIT, SWIGLU_LIMIT)
        act = x_glu * jax.nn.sigmoid(SWIGLU_ALPHA * x_glu) * (x_lin + 1.0)
        y = _dot(act.astype(BF16), w2_ref[...]) + b2_ref[...]
        block_copy(slot, ssem).wait()
        yb_scr[slot] = y

    @pl.when(i == n_used)
    def _():
        block_copy(slot, gsem).wait()
        block_copy(slot, ssem).wait()
        for_rows(lambda r: scatter_start(i - 1, 1 - slot, r))
        block_copy(1 - slot, ssem).wait()


def _moe(fx, route, l, w1_b, b1, w2_b, b2):
    block_e, n_used, rows_tok, rows_dst = route
    grid_spec = pltpu.PrefetchScalarGridSpec(
        num_scalar_prefetch=4,
        grid=(MOE_NBLK,),
        in_specs=[
            pl.BlockSpec(memory_space=pl.ANY),
            pl.BlockSpec((None, None, D_MODEL, 2 * D_FF), lambda i, be, *_: (l, be[i], 0, 0)),
            pl.BlockSpec((None, None, 1, 2 * D_FF), lambda i, be, *_: (l, be[i], 0, 0)),
            pl.BlockSpec((None, None, D_FF, D_MODEL), lambda i, be, *_: (l, be[i], 0, 0)),
            pl.BlockSpec((None, None, 1, D_MODEL), lambda i, be, *_: (l, be[i], 0, 0)),
        ],
        out_specs=pl.BlockSpec(memory_space=pl.ANY),
        scratch_shapes=[pltpu.VMEM((2, MOE_BLOCK, D_MODEL), F32),
                        pltpu.VMEM((2, MOE_BLOCK, D_MODEL), F32),
                        pltpu.SemaphoreType.DMA((2,)),
                        pltpu.SemaphoreType.DMA((2,))],
    )
    return pl.pallas_call(
        _moe_kernel,
        out_shape=jax.ShapeDtypeStruct((MOE_OUT_ROWS, D_MODEL), F32),
        grid_spec=grid_spec,
        compiler_params=_params(), name="moe",
    )(block_e, n_used, rows_tok, rows_dst, fx, w1_b,
      b1.reshape(DEPTH, N_EXPERTS, 1, 2 * D_FF), w2_b, b2.reshape(DEPTH, N_EXPERTS, 1, D_MODEL))


def _combine_kernel(final, tile0, y0_ref, y1_ref, y2_ref, y3_ref, tg_ref, xs_ref, mod_ref, fn_ref,
                    o_ref):
    is_ctx = _row_is_ctx(pl.program_id(0) + tile0, TILE)
    f = None
    for k, y_ref in enumerate((y0_ref, y1_ref, y2_ref, y3_ref)):
        yk = y_ref[...] * tg_ref[:, k:k + 1]
        f = yk if f is None else f + yk
    xs = xs_ref[...] + _mod_rows(mod_ref, 5, is_ctx) * f
    if final:
        xs = _rms_rows(xs) * fn_ref[...]
    o_ref[...] = xs


def _combine(y4, top_g, xs, mod_l, final_norm, final):
    tile0 = CTX_LEN // TILE if final else 0
    n_out = SEQ if final else N_TOK
    return pl.pallas_call(
        functools.partial(_combine_kernel, final, tile0),
        out_shape=jax.ShapeDtypeStruct((n_out, D_MODEL), F32),
        grid=(n_out // TILE,),
        in_specs=[pl.BlockSpec((TILE, D_MODEL), functools.partial(lambda k, i: (k * N_TILES + i + tile0, 0), k))
                  for k in range(TOP_K)] + [
                  pl.BlockSpec((TILE, LANES), lambda i: (i + tile0, 0)),
                  pl.BlockSpec((TILE, D_MODEL), lambda i: (i + tile0, 0)),
                  pl.BlockSpec((SUBLANES, 6 * D_MODEL), lambda i: (0, 0)),
                  pl.BlockSpec((1, D_MODEL), lambda i: (0, 0))],
        out_specs=pl.BlockSpec((TILE, D_MODEL), lambda i: (i, 0)),
        compiler_params=_params(), name="combine_final" if final else "combine",
    )(y4, y4, y4, y4, top_g, xs, mod_l, final_norm.reshape(1, D_MODEL))


def _w_in_layout(w):
    g = GROUP_W
    s5, ret, lru, rw = w[:, :g], w[:, g:5 * g], w[:, 5 * g:7 * g], w[:, 7 * g:]
    z = jnp.zeros((D_MODEL, LANES - RWKV_RANK), w.dtype)
    rw = jnp.concatenate([rw[:, :3 * g], rw[:, 3 * g:3 * g + RWKV_RANK], z,
                          rw[:, 3 * g + RWKV_RANK:3 * g + 2 * RWKV_RANK], z,
                          rw[:, 3 * g + 2 * RWKV_RANK:]], axis=1)
    return jnp.concatenate([rw, s5, ret, lru], axis=1).astype(BF16)


def kernel(x, c, ctx, c_ctx, w_ada, b_ada, norm1, norm2, w_in, w_out, mix_gain, s5_lam_re, s5_lam_im, s5_log_dt, s5_b_re, s5_b_im, s5_c_re, s5_c_im, s5_d, s5_w_glu, s5_b_glu, ret_decay, lru_conv_w, lru_conv_b, lru_lam, lru_w_r, lru_b_r, lru_w_i, lru_b_i, rwkv_mu, rwkv_w0, rwkv_w_up, rwkv_a0, rwkv_a_up, rwkv_g_up, rwkv_k_k, rwkv_k_a, rwkv_r_k, rwkv_ln_w, rwkv_ln_b, router_w, router_b, exp_w1, exp_b1, exp_w2, exp_b2, final_norm):
    assert x.shape == (1, SEQ, D_MODEL) and ctx.shape == (1, CTX_LEN, D_MODEL)
    xs = jnp.concatenate([ctx[0], x[0]], axis=0).astype(F32)
    cc = jnp.zeros((SUBLANES, D_MODEL), F32).at[0].set(c[0]).at[1].set(c_ctx)
    mods = _modulation(cc, w_ada, b_ada)
    rope = _rope_tables()
    w1_b, w2_b = exp_w1.astype(BF16), exp_w2.astype(BF16)

    for l in range(DEPTH):
        mod_l = mods[l]
        proj = _inproj(xs, norm1[l], mod_l, _w_in_layout(w_in[l]))

        s5_tabs = [_s5_tables(s5_lam_re[l, d], s5_lam_im[l, d], s5_log_dt[l, d], s5_b_re[l, d],
                              s5_b_im[l, d], s5_c_re[l, d], s5_c_im[l, d], d == 1) for d in range(2)]
        ya = _s5_mixer(proj, s5_tabs[0], s5_tabs[1], s5_d[l], s5_w_glu[l], s5_b_glu[l])

        ret_f, ret_b = _ret_tables(ret_decay[l])
        yb = _ret_mixer(proj, rope, ret_f, ret_b)

        yc = _lru_mixer(proj, lru_conv_w[l], lru_conv_b[l], lru_lam[l], lru_w_r[l], lru_b_r[l],
                        lru_w_i[l], lru_b_i[l])

        r, v, kk, bonus, g, w, kd, b = _rwkv_prep(
            proj, rwkv_mu[l], rwkv_k_k[l], rwkv_k_a[l], rwkv_r_k[l], rwkv_w0[l], rwkv_w_up[l],
            rwkv_a0[l], rwkv_a_up[l], rwkv_g_up[l])
        ydf, ydb = _rwkv_chunks(r, v, kk, w, kd, b)

        xs, fx, top_e, top_g = _merge(ya, yb, yc, ydf, ydb, bonus, g, rwkv_ln_w[l], rwkv_ln_b[l],
                                      mix_gain[l], w_out[l].astype(BF16), xs, mod_l, norm2[l],
                                      router_w[l], router_b[l])
        route = _route(top_e[:, :TOP_K])
        y4 = _moe(fx, route, l, w1_b, exp_b1, w2_b, exp_b2)
        xs = _combine(y4, top_g, xs, mod_l, final_norm, l == DEPTH - 1)

    return xs.reshape(1, SEQ, D_MODEL)
```

```python
import functools
import math

import jax
import jax.numpy as jnp
from jax import lax
from jax.experimental import pallas as pl
from jax.experimental.pallas import tpu as pltpu

F32 = jnp.float32
BF16 = jnp.bfloat16

D_MODEL = 2048
SEQ = 8192
CTX_LEN = 256
N_TOK = SEQ + CTX_LEN
DEPTH = 4
GRID_W = 64
GROUP_W = 512
NORM_EPS = 1e-6
S5_CH = 16
S5_GROUPS = 32
S5_STATE = 64
RET_HEADS = 4
RET_DK = 128
RET_CHUNK = 128
ROPE_BASE = 10000.0
LRU_BLOCKS = 8
LRU_BW = 64
LRU_CONV = 4
LRU_C = 8.0
RWKV_HEAD = 64
RWKV_HEADS = 8
RWKV_RANK = 96
RWKV_GATE_RANK = 256
RWKV_LN_EPS = 64e-5
N_EXPERTS = 32
TOP_K = 4
D_FF = 896
SWIGLU_ALPHA = 1.702
SWIGLU_LIMIT = 7.0
MOE_BLOCK = 256

LANES = 128
SUBLANES = 8
VMEM_LIMIT = 56 * 1024 * 1024

N_PROJ = 5632
COL_RWKV = 0
COL_S5 = 2048
COL_RET = 2560
COL_LRU = 4608

TILE = 256
N_TILES = N_TOK // TILE
S5_SEG = 32
RW_TB = 64
RW_TILES = N_TOK // RW_TB
RW_CTX_TILES = CTX_LEN // RW_TB
RET_TILES = N_TOK // RET_CHUNK
RET_CTX_TILES = CTX_LEN // RET_CHUNK


def _params(n_axes=1):
    return pltpu.CompilerParams(
        dimension_semantics=("arbitrary",) * n_axes, vmem_limit_bytes=VMEM_LIMIT)


def _dot(a, b):
    return jnp.dot(a, b, preferred_element_type=F32)


def _split2(x):
    hi = x.astype(BF16)
    lo = (x - hi.astype(F32)).astype(BF16)
    return hi, lo


def _dot_x3(a, b):
    ah, al = _split2(a)
    bh, bl = _split2(b)
    return _dot(ah, bh) + _dot(ah, bl) + _dot(al, bh)


def _dot_sel(a, sel):
    ah, al = _split2(a)
    return _dot(ah, sel) + _dot(al, sel)


def _sel_dot(sel, a):
    ah, al = _split2(a)
    return _dot(sel, ah) + _dot(sel, al)


def _bwd_tile(s, n_ctx, n_all):
    return jnp.where(s < n_ctx, n_ctx - 1 - s, n_all - 1 - (s - n_ctx))


def _rms_rows(x):
    return x * lax.rsqrt(jnp.mean(x * x, axis=-1, keepdims=True) + NORM_EPS)


def _row_is_ctx(tile_idx, tile_rows):
    row = tile_idx * tile_rows + lax.broadcasted_iota(jnp.int32, (tile_rows, 1), 0)
    return row < CTX_LEN


def _mod_kernel(cc_ref, w_ref, b_ref, o_ref):
    s = cc_ref[...]
    s = s * jax.nn.sigmoid(s)
    o_ref[...] = _dot(s.astype(BF16), w_ref[...].astype(BF16)) + b_ref[...]


def _modulation(cc, w_ada, b_ada):
    tn = 1024
    nb = 6 * D_MODEL // tn
    return pl.pallas_call(
        _mod_kernel,
        out_shape=jax.ShapeDtypeStruct((DEPTH, SUBLANES, 6 * D_MODEL), F32),
        grid=(DEPTH, nb),
        in_specs=[
            pl.BlockSpec((SUBLANES, D_MODEL), lambda l, j: (0, 0)),
            pl.BlockSpec((None, D_MODEL, tn), lambda l, j: (l, 0, j)),
            pl.BlockSpec((None, 1, tn), lambda l, j: (l, 0, j)),
        ],
        out_specs=pl.BlockSpec((None, SUBLANES, tn), lambda l, j: (l, 0, j)),
        compiler_params=_params(2),
        name="modulation",
    )(cc, w_ada, b_ada.reshape(DEPTH, 1, 6 * D_MODEL))


def _mod_rows(mod_ref, chunk, is_ctx):
    lo, hi = chunk * D_MODEL, (chunk + 1) * D_MODEL
    return jnp.where(is_ctx, mod_ref[1:2, lo:hi], mod_ref[0:1, lo:hi])


IN_TM = 768
IN_TN = 512


def _inproj_kernel(x_ref, g_ref, mod_ref, w_ref, o_ref, h_scr):
    i, j = pl.program_id(0), pl.program_id(1)

    @pl.when(j == 0)
    def _():
        is_ctx = _row_is_ctx(i, IN_TM)
        xn = _rms_rows(x_ref[...]) * g_ref[...]
        h = xn * (1.0 + _mod_rows(mod_ref, 1, is_ctx)) + _mod_rows(mod_ref, 0, is_ctx)
        h_scr[...] = h.astype(BF16)

    o_ref[...] = _dot(h_scr[...], w_ref[...])


def _inproj(xs, norm_g, mod_l, w_in_b):
    return pl.pallas_call(
        _inproj_kernel,
        out_shape=jax.ShapeDtypeStruct((N_TOK, N_PROJ), F32),
        grid=(N_TOK // IN_TM, N_PROJ // IN_TN),
        in_specs=[
            pl.BlockSpec((IN_TM, D_MODEL), lambda i, j: (i, 0)),
            pl.BlockSpec((1, D_MODEL), lambda i, j: (0, 0)),
            pl.BlockSpec((SUBLANES, 6 * D_MODEL), lambda i, j: (0, 0)),
            pl.BlockSpec((D_MODEL, IN_TN), lambda i, j: (0, j)),
        ],
        out_specs=pl.BlockSpec((IN_TM, IN_TN), lambda i, j: (i, j)),
        scratch_shapes=[pltpu.VMEM((IN_TM, D_MODEL), BF16)],
        compiler_params=_params(2),
        name="inproj",
    )(xs, norm_g.reshape(1, D_MODEL), mod_l, w_in_b)


S5_NQ = 4
S5_QW = 2 * 8 * S5_STATE
S5_W = S5_NQ * S5_QW


def _s5_swap(h):
    half = S5_QW // 2
    parts = []
    for q in range(S5_NQ):
        parts.append(h[:, q * S5_QW + half:(q + 1) * S5_QW])
        parts.append(h[:, q * S5_QW:q * S5_QW + half])
    return jnp.concatenate(parts, axis=1)


def _s5_kernel(rev, *refs):
    if rev:
        (u_ref, yf_ref, perm_ref, permt_ref, bblk_ref, cblk_ref, a_ref, pw_ref,
         dskip_ref, wglu_ref, bglu_ref, o_ref, carry_scr, bu_scr, hl_scr, hs_scr) = refs
    else:
        (u_ref, perm_ref, permt_ref, bblk_ref, cblk_ref, a_ref, pw_ref,
         o_ref, carry_scr, bu_scr, hl_scr, hs_scr) = refs

    @pl.when(pl.program_id(0) == 0)
    def _():
        carry_scr[...] = jnp.zeros_like(carry_scr)

    u = u_ref[...]
    up = _dot(perm_ref[...], u.astype(BF16)).astype(BF16)
    for q in range(S5_NQ):
        bu = _dot(up[:, q * LANES:(q + 1) * LANES], bblk_ref[q])
        bu_scr[:, :, q * S5_QW:(q + 1) * S5_QW] = bu.reshape(S5_SEG, SUBLANES, S5_QW)

    a1, a2 = a_ref[0:1, :], a_ref[1:2, :]
    at1, at2 = a_ref[2:3, :], a_ref[3:4, :]

    def step(s, h):
        p = S5_SEG - 1 - s if rev else s
        h = a1 * h + a2 * _s5_swap(h) + bu_scr[p]
        hl_scr[p] = h
        return h

    ends = lax.fori_loop(0, S5_SEG, step, jnp.zeros((SUBLANES, S5_W), F32))

    c = carry_scr[...]
    for j in (range(SUBLANES - 1, -1, -1) if rev else range(SUBLANES)):
        hs_scr[j:j + 1, :] = c
        c = ends[j:j + 1, :] + at1 * c + at2 * _s5_swap(c)
    carry_scr[...] = c

    hs = hs_scr[...]
    hsw = _s5_swap(hs)

    def fix(p8, carry):
        base = pl.multiple_of(p8 * SUBLANES, SUBLANES)
        pw1 = pw_ref[0, pl.ds(base, SUBLANES), :]
        pw2 = pw_ref[1, pl.ds(base, SUBLANES), :]
        for tt in range(SUBLANES):
            p = base + tt
            hl_scr[p] = hl_scr[p] + pw1[tt:tt + 1, :] * hs + pw2[tt:tt + 1, :] * hsw
        return carry

    lax.fori_loop(0, S5_SEG // SUBLANES, fix, 0)

    ys = []
    for q in range(S5_NQ):
        hq = hl_scr[:, :, q * S5_QW:(q + 1) * S5_QW].reshape(TILE, S5_QW)
        ys.append(_dot(hq.astype(BF16), cblk_ref[q]))
    y = _sel_dot(permt_ref[...], jnp.concatenate(ys, axis=1))

    if rev:
        y = y + yf_ref[...] + dskip_ref[...] * u
        y = jax.nn.gelu(y)
        gate = _dot(y.astype(BF16), wglu_ref[...]) + bglu_ref[...]
        o_ref[...] = y * jax.nn.sigmoid(gate)
    else:
        o_ref[...] = y


def _s5_tables(lam_re, lam_im, log_dt, b_re, b_im, c_re, c_im, rev):
    dt = jnp.exp(log_dt.astype(F32))[:, None]
    lr = jnp.minimum(lam_re.astype(F32), -1e-4)
    li = lam_im.astype(F32)

    def a_pow(k):
        mag = jnp.exp(k * lr * dt)
        return mag * jnp.cos(k * li * dt), mag * jnp.sin(k * li * dt)

    def cols(re, im):
        lead = re.shape[:-2]
        re = re.reshape(lead + (S5_NQ, 8 * S5_STATE))
        im = im.reshape(lead + (S5_NQ, 8 * S5_STATE))
        return jnp.concatenate([re, im], axis=-1).reshape(lead + (S5_W,))

    ab_re, ab_im = a_pow(1.0)
    den = lr * lr + li * li
    f_re = ((ab_re - 1.0) * lr + ab_im * li) / den
    f_im = (ab_im * lr - (ab_re - 1.0) * li) / den
    bb_re = f_re[..., None] * b_re - f_im[..., None] * b_im
    bb_im = f_re[..., None] * b_im + f_im[..., None] * b_re
    at_re, at_im = a_pow(float(S5_SEG))
    a_tab = jnp.stack([cols(ab_re, ab_re), cols(-ab_im, ab_im),
                       cols(at_re, at_re), cols(-at_im, at_im)])
    ks = jnp.arange(S5_SEG, dtype=F32)
    ks = (S5_SEG - ks) if rev else (ks + 1.0)
    pk_re, pk_im = a_pow(ks[:, None, None])
    pw = jnp.stack([cols(pk_re, pk_re), cols(-pk_im, pk_im)])

    eye = jnp.eye(8, dtype=F32)

    def bdiag(m):
        a, b = m.shape[1:]
        m4 = m.reshape(S5_NQ, 8, a, b)
        return jnp.einsum('qgab,gh->qgahb', m4, eye).reshape(S5_NQ, 8 * a, 8 * b)

    bblk = jnp.concatenate([bdiag(jnp.swapaxes(bb_re, 1, 2)),
                            bdiag(jnp.swapaxes(bb_im, 1, 2))], axis=2).astype(BF16)
    cblk = jnp.concatenate([bdiag(jnp.swapaxes(c_re.astype(F32), 1, 2)),
                            bdiag(-jnp.swapaxes(c_im.astype(F32), 1, 2))], axis=1).astype(BF16)
    return a_tab, pw, bblk, cblk


def _s5_perm():
    r = jnp.arange(TILE)
    src = (r % SUBLANES) * S5_SEG + r // SUBLANES
    perm = (src[:, None] == jnp.arange(TILE)[None, :]).astype(BF16)
    return perm, perm.T


def _s5_mixer(proj, tabs_f, tabs_b, d_skip, w_glu, b_glu):
    perm, permt = _s5_perm()
    col = COL_S5 // GROUP_W

    def const(shape):
        return pl.BlockSpec(shape, lambda s: (0,) * len(shape))

    tab_specs = [const((TILE, TILE)), const((TILE, TILE)),
                 const((S5_NQ, LANES, S5_QW)), const((S5_NQ, S5_QW, LANES)),
                 const((4, S5_W)), const((2, S5_SEG, S5_W))]
    scratch = [pltpu.VMEM((1, S5_W), F32),
               pltpu.VMEM((S5_SEG, SUBLANES, S5_W), F32),
               pltpu.VMEM((S5_SEG, SUBLANES, S5_W), F32),
               pltpu.VMEM((SUBLANES, S5_W), F32)]
    out_shape = jax.ShapeDtypeStruct((N_TOK, GROUP_W), F32)

    a_tab, pw, bblk, cblk = tabs_f
    yf = pl.pallas_call(
        functools.partial(_s5_kernel, False),
        out_shape=out_shape, grid=(N_TILES,),
        in_specs=[pl.BlockSpec((TILE, GROUP_W), lambda s: (s, col))] + tab_specs,
        out_specs=pl.BlockSpec((TILE, GROUP_W), lambda s: (s, 0)),
        scratch_shapes=scratch, compiler_params=_params(), name="s5_fwd",
    )(proj, perm, permt, bblk, cblk, a_tab, pw)

    a_tab, pw, bblk, cblk = tabs_b
    bt = lambda s: _bwd_tile(s, 1, N_TILES)
    return pl.pallas_call(
        functools.partial(_s5_kernel, True),
        out_shape=out_shape, grid=(N_TILES,),
        in_specs=[pl.BlockSpec((TILE, GROUP_W), lambda s: (bt(s), col)),
                  pl.BlockSpec((TILE, GROUP_W), lambda s: (bt(s), 0))] + tab_specs
                 + [const((1, GROUP_W)), const((GROUP_W, GROUP_W)), const((1, GROUP_W))],
        out_specs=pl.BlockSpec((TILE, GROUP_W), lambda s: (bt(s), 0)),
        scratch_shapes=scratch, compiler_params=_params(), name="s5_bwd",
    )(proj, yf, perm, permt, bblk, cblk, a_tab, pw,
      d_skip.reshape(1, GROUP_W), w_glu.astype(BF16), b_glu.reshape(1, GROUP_W))


def _rope_shuffle(x):
    lane = lax.broadcasted_iota(jnp.int32, x.shape, 1)
    return jnp.where(lane % 64 < 32, pltpu.roll(x, LANES - 32, 1), pltpu.roll(x, 32, 1))


def _ret_kernel(rev, *refs):
    if rev:
        (q_ref, k_ref, v_ref, g_ref, of_ref, cos_ref, sin_ref, dintra_ref, dq_ref, dk_ref,
         dc_ref, o_ref, s_scr) = refs
    else:
        (q_ref, k_ref, v_ref, cos_ref, sin_ref, dintra_ref, dq_ref, dk_ref,
         dc_ref, o_ref, s_scr) = refs

    @pl.when(pl.program_id(0) == 0)
    def _():
        s_scr[...] = jnp.zeros_like(s_scr)

    cos, sin = cos_ref[...], sin_ref[...]
    outs = []
    for h in range(RET_HEADS):
        sl = slice(h * RET_DK, (h + 1) * RET_DK)
        qh = q_ref[:, sl]
        kh = k_ref[:, sl] * (RET_DK ** -0.5)
        qh = qh * cos + _rope_shuffle(qh) * sin
        kh = kh * cos + _rope_shuffle(kh) * sin
        qb, kb, vb = qh.astype(BF16), kh.astype(BF16), v_ref[:, sl].astype(BF16)
        sc = lax.dot_general(qb, kb, (((1,), (1,)), ((), ())),
                             preferred_element_type=F32) * dintra_ref[h]
        s_old = s_scr[h]
        o = _dot(sc.astype(BF16), vb) + _dot(qb, s_old.astype(BF16)) * dq_ref[h]
        kt = (kh * dk_ref[h]).T.astype(BF16)
        s_scr[h] = s_old * dc_ref[h] + _dot(kt, vb)
        if rev:
            o = o + of_ref[:, sl]
            gh = g_ref[:, sl]
            o = o * lax.rsqrt(jnp.mean(o * o, axis=-1, keepdims=True) + NORM_EPS)
            o = o * (gh * jax.nn.sigmoid(gh))
        outs.append(o)
    o_ref[...] = jnp.concatenate(outs, axis=1)


def _ret_tables(decay):
    log_g = -jnp.exp(decay.astype(F32))
    idx = jnp.arange(RET_CHUNK, dtype=F32)
    rel = idx[:, None] - idx[None, :]
    c = float(RET_CHUNK)
    lf = log_g[0][:, None, None]
    lb = log_g[1][:, None, None]
    d_f = jnp.where((rel >= 0)[None], jnp.exp(jnp.maximum(rel, 0.0)[None] * lf), 0.0)
    d_b = jnp.where((rel < 0)[None], jnp.exp(jnp.maximum(-rel, 0.0)[None] * lb), 0.0)
    ones = jnp.ones((1, 1, RET_DK), F32)
    dq_f = jnp.exp((idx + 1.0)[None, :, None] * lf) * ones
    dk_f = jnp.exp((c - 1.0 - idx)[None, :, None] * lf) * ones
    dq_b = jnp.exp((c - idx)[None, :, None] * lb) * ones
    dk_b = jnp.exp(idx[None, :, None] * lb) * ones
    dc_f = jnp.exp(c * lf) * ones
    dc_b = jnp.exp(c * lb) * ones
    return (d_f, dq_f, dk_f, dc_f), (d_b, dq_b, dk_b, dc_b)


def _rope_tables():
    m = 32
    inv = ROPE_BASE ** (-jnp.arange(m, dtype=F32) / m)
    t = jnp.arange(SEQ)
    ang_r = (t // GRID_W).astype(F32)[:, None] * inv[None, :]
    ang_c = (t % GRID_W).astype(F32)[:, None] * inv[None, :]
    cos = jnp.concatenate([jnp.cos(ang_r)] * 2 + [jnp.cos(ang_c)] * 2, axis=1)
    sin = jnp.concatenate([-jnp.sin(ang_r), jnp.sin(ang_r), -jnp.sin(ang_c), jnp.sin(ang_c)], axis=1)
    cos = jnp.concatenate([jnp.ones((CTX_LEN, RET_DK), F32), cos], axis=0)
    sin = jnp.concatenate([jnp.zeros((CTX_LEN, RET_DK), F32), sin], axis=0)
    return cos, sin


def _ret_mixer(proj, rope, tabs_f, tabs_b):
    cos, sin = rope
    c0 = COL_RET // GROUP_W

    def const(shape):
        return pl.BlockSpec(shape, lambda s: (0,) * len(shape))

    def tab_specs():
        return [const((RET_HEADS, RET_CHUNK, RET_CHUNK)), const((RET_HEADS, RET_CHUNK, RET_DK)),
                const((RET_HEADS, RET_CHUNK, RET_DK)), const((RET_HEADS, 1, RET_DK))]

    out_shape = jax.ShapeDtypeStruct((N_TOK, GROUP_W), F32)
    scratch = [pltpu.VMEM((RET_HEADS, RET_DK, RET_DK), F32)]

    def tok(cb, f):
        return pl.BlockSpec((RET_CHUNK, GROUP_W), lambda s: (f(s), cb))

    def rope_spec(f):
        return pl.BlockSpec((RET_CHUNK, RET_DK), lambda s: (f(s), 0))

    ident = lambda s: s
    of = pl.pallas_call(
        functools.partial(_ret_kernel, False),
        out_shape=out_shape, grid=(RET_TILES,),
        in_specs=[tok(c0, ident), tok(c0 + 1, ident), tok(c0 + 2, ident),
                  rope_spec(ident), rope_spec(ident)] + tab_specs(),
        out_specs=tok(0, ident),
        scratch_shapes=scratch, compiler_params=_params(), name="ret_fwd",
    )(proj, proj, proj, cos, sin, *tabs_f)

    bt = lambda s: _bwd_tile(s, RET_CTX_TILES, RET_TILES)
    return pl.pallas_call(
        functools.partial(_ret_kernel, True),
        out_shape=out_shape, grid=(RET_TILES,),
        in_specs=[tok(c0, bt), tok(c0 + 1, bt), tok(c0 + 2, bt), tok(c0 + 3, bt), tok(0, bt),
                  rope_spec(bt), rope_spec(bt)] + tab_specs(),
        out_specs=tok(0, bt),
        scratch_shapes=scratch, compiler_params=_params(), name="ret_bwd",
    )(proj, proj, proj, proj, of, cos, sin, *tabs_b)


def _lru_kernel(rev, *refs):
    if rev:
        (pv_ref, x_ref, nx_ref, gate_ref, hf_ref, cw_ref, cb_ref, sp_ref, wr_ref, br_ref,
         wi_ref, bi_ref, o_ref, carry_scr) = refs
    else:
        (pv_ref, x_ref, nx_ref, cw_ref, cb_ref, sp_ref, wr_ref, br_ref,
         wi_ref, bi_ref, o_ref, carry_scr) = refs
    s = pl.program_id(0)
    tile = _bwd_tile(s, 1, N_TILES) if rev else s

    @pl.when(s == 0)
    def _():
        carry_scr[...] = jnp.zeros_like(carry_scr)

    has_prev = jnp.logical_and(tile != 0, tile != 1).astype(F32)
    has_next = jnp.logical_and(tile != 0, tile != N_TILES - 1).astype(F32)
    x = x_ref[...]
    pv = pv_ref[...] * has_prev
    nx = nx_ref[...] * has_next
    row = lax.broadcasted_iota(jnp.int32, (TILE, GROUP_W), 0)
    xm1 = jnp.where(row == 0, pv[7:8, :], pltpu.roll(x, 1, 0))
    xm2 = jnp.where(row == 0, pv[6:7, :], jnp.where(row == 1, pv[7:8, :], pltpu.roll(x, 2, 0)))
    xp1 = jnp.where(row == TILE - 1, nx[0:1, :], pltpu.roll(x, TILE - 1, 0))
    xc = (xm2 * cw_ref[0:1, :] + xm1 * cw_ref[1:2, :] + x * cw_ref[2:3, :]
          + xp1 * cw_ref[3:4, :] + cb_ref[...])

    r = jax.nn.sigmoid(_dot_x3(xc, wr_ref[...]) + br_ref[...])
    ig = jax.nn.sigmoid(_dot_x3(xc, wi_ref[...]) + bi_ref[...])
    log_a = -LRU_C * r * sp_ref[...]
    a = jnp.exp(log_a)
    b = jnp.sqrt(1.0 - jnp.exp(2.0 * log_a)) * (ig * xc)

    sh = 1
    while sh < TILE:
        if rev:
            keep = row < TILE - sh
            a_s = jnp.where(keep, pltpu.roll(a, TILE - sh, 0), 1.0)
            b_s = jnp.where(keep, pltpu.roll(b, TILE - sh, 0), 0.0)
        else:
            keep = row >= sh
            a_s = jnp.where(keep, pltpu.roll(a, sh, 0), 1.0)
            b_s = jnp.where(keep, pltpu.roll(b, sh, 0), 0.0)
        b = a * b_s + b
        a = a * a_s
        sh *= 2
    h = b + a * carry_scr[...]
    carry_scr[...] = h[0:1, :] if rev else h[TILE - 1:TILE, :]
    if rev:
        o_ref[...] = (h + hf_ref[...]) * jax.nn.gelu(gate_ref[...])
    else:
        o_ref[...] = h


def _blockdiag(w):
    h, a, b = w.shape
    return jnp.einsum('hab,hg->hagb', w, jnp.eye(h, dtype=w.dtype)).reshape(h * a, h * b)


def _lru_mixer(proj, conv_w, conv_b, lam, w_r, b_r, w_i, b_i):
    cx = COL_LRU // GROUP_W
    rows8 = TILE // SUBLANES
    last8 = N_TOK // SUBLANES - 1

    def const(shape):
        return pl.BlockSpec(shape, lambda s: (0,) * len(shape))

    def specs(f, d):
        return dict(
            pv=pl.BlockSpec((SUBLANES, GROUP_W), lambda s: (jnp.maximum(f(s) * rows8 - 1, 0), cx)),
            x=pl.BlockSpec((TILE, GROUP_W), lambda s: (f(s), cx)),
            nx=pl.BlockSpec((SUBLANES, GROUP_W), lambda s: (jnp.minimum((f(s) + 1) * rows8, last8), cx)),
            gate=pl.BlockSpec((TILE, GROUP_W), lambda s: (f(s), cx + 1)),
            out=pl.BlockSpec((TILE, GROUP_W), lambda s: (f(s), 0)),
        )

    def dir_params(d):
        return (conv_w, conv_b.reshape(1, GROUP_W),
                jax.nn.softplus(-lam[d].astype(F32)).reshape(1, GROUP_W),
                _blockdiag(w_r[d]), b_r[d].reshape(1, GROUP_W),
                _blockdiag(w_i[d]), b_i[d].reshape(1, GROUP_W))

    par_specs = [const((LRU_CONV, GROUP_W)), const((1, GROUP_W)), const((1, GROUP_W)),
                 const((GROUP_W, GROUP_W)), const((1, GROUP_W)),
                 const((GROUP_W, GROUP_W)), const((1, GROUP_W))]
    out_shape = jax.ShapeDtypeStruct((N_TOK, GROUP_W), F32)
    scratch = [pltpu.VMEM((1, GROUP_W), F32)]

    sp = specs(lambda s: s, 0)
    hf = pl.pallas_call(
        functools.partial(_lru_kernel, False),
        out_shape=out_shape, grid=(N_TILES,),
        in_specs=[sp['pv'], sp['x'], sp['nx']] + par_specs,
        out_specs=sp['out'], scratch_shapes=scratch, compiler_params=_params(), name="lru_fwd",
    )(proj, proj, proj, *dir_params(0))

    sp = specs(lambda s: _bwd_tile(s, 1, N_TILES), 1)
    return pl.pallas_call(
        functools.partial(_lru_kernel, True),
        out_shape=out_shape, grid=(N_TILES,),
        in_specs=[sp['pv'], sp['x'], sp['nx'], sp['gate'], sp['out']] + par_specs,
        out_specs=sp['out'], scratch_shapes=scratch, compiler_params=_params(), name="lru_bwd",
    )(proj, proj, proj, proj, hf, *dir_params(1))


RW_IN = 2048
RW_HALO = 64


def _head_ones():
    h = jnp.arange(GROUP_W) // RWKV_HEAD
    return (h[:, None] == h[None, :]).astype(BF16)


def _rwkv_prep_kernel(pv_ref, cur_ref, nx_ref, mu_ref, kk_ref, ka_ref, rk_ref, w0_ref, wup_ref,
                      a0_ref, aup_ref, gup_ref, ones_ref,
                      r_o, v_o, kkn_o, bonus_o, g_o, w_o, kd_o, b_o, ext_scr, z_scr):
    i = pl.program_id(0)
    is_ctx = i == 0
    ext_scr[0:RW_HALO, :] = pv_ref[...]
    ext_scr[RW_HALO:RW_HALO + TILE, :] = cur_ref[...]
    ext_scr[RW_HALO + TILE:, :] = nx_ref[...]

    row = lax.broadcasted_iota(jnp.int32, (TILE, LANES), 0)
    lane = lax.broadcasted_iota(jnp.int32, (TILE, LANES), 1)
    c4 = lane % 4
    one = jnp.ones((TILE, LANES), F32)
    zero = jnp.zeros((TILE, LANES), F32)

    def mask(c):
        return jnp.where(c, one, zero)

    up_rows = jnp.where(i == 1, mask(row >= RW_HALO), one)
    dn_rows = jnp.where(i == N_TILES - 1, mask(row < TILE - RW_HALO), one)
    m_up = jnp.where(is_ctx, zero, up_rows * mask(c4 == 0))
    m_dn = jnp.where(is_ctx, zero, dn_rows * mask(c4 == 1))
    m_lt = jnp.where(is_ctx, mask(row >= 1) * mask(c4 % 2 == 0),
                     mask(row % GRID_W != 0) * mask(c4 == 2))
    m_rt = jnp.where(is_ctx, mask(row <= TILE - 2) * mask(c4 % 2 == 1),
                     mask(row % GRID_W != GRID_W - 1) * mask(c4 == 3))

    for cb in range(RW_IN // LANES):
        sl = slice(cb * LANES, (cb + 1) * LANES)
        p = ext_scr[RW_HALO:RW_HALO + TILE, sl]
        shifted = (ext_scr[0:TILE, sl] * m_up
                   + ext_scr[2 * RW_HALO:2 * RW_HALO + TILE, sl] * m_dn
                   + ext_scr[RW_HALO - 1:RW_HALO - 1 + TILE, sl] * m_lt
                   + ext_scr[RW_HALO + 1:RW_HALO + 1 + TILE, sl] * m_rt)
        z_scr[:, sl] = p + (shifted - p) * mu_ref[:, sl]

    gw = GROUP_W
    r = z_scr[:, 0:gw]
    k = z_scr[:, gw:2 * gw]
    v = z_scr[:, 2 * gw:3 * gw]
    wc = z_scr[:, 3 * gw:3 * gw + LANES]
    ac = z_scr[:, 3 * gw + LANES:3 * gw + 2 * LANES]
    gc = z_scr[:, 3 * gw + 2 * LANES:]
    ones = ones_ref[...]

    r_o[...] = r
    v_o[...] = v
    g_o[...] = _dot(jax.nn.sigmoid(gc).astype(BF16), gup_ref[...])
    kk = k * kk_ref[...]
    ss = _dot_sel(kk * kk, ones)
    kk = kk * lax.rsqrt(jnp.maximum(ss, 1e-12))
    kkn_o[...] = kk
    bonus_o[...] = _dot_sel(r * k * rk_ref[...], ones) * v
    tw = jnp.tanh(wc)
    for d in range(2):
        w_log = -jax.nn.softplus(-(w0_ref[d] + _dot_x3(tw, wup_ref[d]))) - 0.5
        w_o[d] = -jnp.exp(w_log)
        a = jax.nn.sigmoid(a0_ref[d] + _dot_x3(ac, aup_ref[d]))
        kd_o[d] = k * (1.0 + (a - 1.0) * ka_ref[...])
        b_o[d] = kk * a


def _rwkv_prep(proj, mu, k_k, k_a, r_k, w0, w_up, a0, a_up, g_up):
    blk64 = TILE // RW_HALO
    last64 = N_TOK // RW_HALO - 1

    def const(shape):
        return pl.BlockSpec(shape, lambda i: (0,) * len(shape))

    tok = pl.BlockSpec((TILE, GROUP_W), lambda i: (i, 0))
    tok2 = pl.BlockSpec((2, TILE, GROUP_W), lambda i: (0, i, 0))
    one = jax.ShapeDtypeStruct((N_TOK, GROUP_W), F32)
    two = jax.ShapeDtypeStruct((2, N_TOK, GROUP_W), F32)
    pad = LANES - RWKV_RANK

    def pad_rows(w):
        return jnp.pad(w.astype(F32), ((0, 0), (0, pad), (0, 0)))

    def pad_mu(m):
        z = jnp.zeros((pad,), F32)
        g3 = 3 * GROUP_W
        return jnp.concatenate([m[:g3], m[g3:g3 + RWKV_RANK], z,
                                m[g3 + RWKV_RANK:g3 + 2 * RWKV_RANK], z,
                                m[g3 + 2 * RWKV_RANK:]]).reshape(1, RW_IN)

    return pl.pallas_call(
        _rwkv_prep_kernel,
        out_shape=(one, one, one, one, one, two, two, two),
        grid=(N_TILES,),
        in_specs=[
            pl.BlockSpec((RW_HALO, RW_IN), lambda i: (jnp.maximum(i * blk64 - 1, 0), 0)),
            pl.BlockSpec((TILE, RW_IN), lambda i: (i, 0)),
            pl.BlockSpec((RW_HALO, RW_IN), lambda i: (jnp.minimum((i + 1) * blk64, last64), 0)),
            const((1, RW_IN)), const((1, GROUP_W)), const((1, GROUP_W)), const((1, GROUP_W)),
            const((2, 1, GROUP_W)), const((2, LANES, GROUP_W)),
            const((2, 1, GROUP_W)), const((2, LANES, GROUP_W)),
            const((RWKV_GATE_RANK, GROUP_W)), const((GROUP_W, GROUP_W)),
        ],
        out_specs=(tok, tok, tok, tok, tok, tok2, tok2, tok2),
        scratch_shapes=[pltpu.VMEM((TILE + 2 * RW_HALO, RW_IN), F32),
                        pltpu.VMEM((TILE, RW_IN), F32)],
        compiler_params=_params(), name="rwkv_prep",
    )(proj, proj, proj, pad_mu(mu.astype(F32)), k_k.reshape(1, GROUP_W), k_a.reshape(1, GROUP_W),
      r_k.reshape(1, GROUP_W), w0.reshape(2, 1, GROUP_W), pad_rows(w_up),
      a0.reshape(2, 1, GROUP_W), pad_rows(a_up), g_up.astype(BF16), _head_ones())


def _rwkv_scan_kernel(rf, vf, kf, wf, kdf, bf, rb, vb, kb, wb, kdb, bb, eye_ref, bd_ref,
                      yf_o, yb_o, s_scr, vexp_scr, yr_scr):
    @pl.when(pl.program_id(0) == 0)
    def _():
        s_scr[...] = jnp.zeros_like(s_scr)

    n_pair = GROUP_W // LANES
    dirs = ((rf, vf, kf, wf, kdf, bf), (rb, vb, kb, wb, kdb, bb))
    eye_b = eye_ref[...].astype(BF16)
    bd = bd_ref[...]

    for d in range(2):
        v_ref = dirs[d][1]
        for hq in range(GROUP_W // RW_BW):
            sl = slice(hq * RW_BW, (hq + 1) * RW_BW)
            vp = v_ref[:, sl].astype(BF16)
            lhs = (vp[:, None, :] * eye_b[None]).reshape(RW_TB * RWKV_HEAD, RW_BW)
            vexp_scr[d, :, :, sl] = _dot(lhs, bd).reshape(RW_TB, RWKV_HEAD, RW_BW)

    lane = lax.broadcasted_iota(jnp.int32, (RWKV_HEAD, LANES), 1)
    low = lane < RWKV_HEAD

    n_grp = RW_TB // SUBLANES

    def step(sg, carry):
        bases = (pl.multiple_of(sg * SUBLANES, SUBLANES),
                 pl.multiple_of((n_grp - 1 - sg) * SUBLANES, SUBLANES))
        rows = [[[ref[pl.ds(bases[d], SUBLANES), pr * LANES:(pr + 1) * LANES]
                  for ref in (dirs[d][0],) + dirs[d][2:]]
                 for pr in range(n_pair)] for d in range(2)]
        for tt in range(SUBLANES):
            for d in range(2):
                ti = tt if d == 0 else SUBLANES - 1 - tt
                t = bases[d] + ti
                for pr in range(n_pair):
                    sl = slice(pr * LANES, (pr + 1) * LANES)
                    r_t, kk_t, w_t, kd_t, b_t = [a[ti:ti + 1, :] for a in rows[d][pr]]
                    st = s_scr[d, :, sl]
                    if d == 1:
                        yr_scr[d, t, :, sl] = (st * r_t).astype(BF16)
                    x = st * kk_t
                    tot = jnp.sum(x, axis=1, keepdims=True)
                    lo = jnp.sum(jnp.where(low, x, 0.0), axis=1, keepdims=True)
                    sa = jnp.where(low, lo, tot - lo)
                    st = st * w_t - sa * b_t + vexp_scr[d, t, :, sl] * kd_t
                    s_scr[d, :, sl] = st
                    if d == 0:
                        yr_scr[d, t, :, sl] = (st * r_t).astype(BF16)
        return carry

    lax.fori_loop(0, n_grp, step, 0)

    eye_f = eye_ref[...]
    for d, y_o in enumerate((yf_o, yb_o)):
        for hq in range(GROUP_W // RW_BW):
            sl = slice(hq * RW_BW, (hq + 1) * RW_BW)
            ys = _dot(yr_scr[d, :, :, sl].reshape(RW_TB * RWKV_HEAD, RW_BW), bd)
            y_o[:, sl] = jnp.sum(ys.reshape(RW_TB, RWKV_HEAD, RW_BW) * eye_f[None], axis=1)


RW_C = 64
RW_T2 = 256
RW_NP = 2


def _split3(x):
    h1 = x.astype(BF16)
    h2 = (x - h1.astype(F32)).astype(BF16)
    h3 = (x - h1.astype(F32) - h2.astype(F32)).astype(BF16)
    return h1, h2, h3


def _rwkv_chunk_kernel(rf, vf, kf, lwf, kdf, bf, rb, vb, kb, lwb, kdb, bb, yf_o, yb_o, st_scr):
    @pl.when(pl.program_id(1) == 0)
    def _():
        st_scr[...] = jnp.zeros_like(st_scr)

    c = RW_C
    n_ch = RW_T2 // c
    ti = lax.broadcasted_iota(jnp.int32, (c, c), 0)
    si = lax.broadcasted_iota(jnp.int32, (c, c), 1)
    lane = lax.broadcasted_iota(jnp.int32, (1, LANES), 1)
    hmask = [jnp.where(lane < RWKV_HEAD, 1.0, 0.0), jnp.where(lane >= RWKV_HEAD, 1.0, 0.0)]
    bi = lax.broadcasted_iota(jnp.int32, (LANES, LANES), 0) // RWKV_HEAD
    bj = lax.broadcasted_iota(jnp.int32, (LANES, LANES), 1) // RWKV_HEAD
    bdmask = jnp.where(bi == bj, 1.0, 0.0)
    eye = jnp.where(ti == si, 1.0, 0.0)

    def nt(a, b):
        return lax.dot_general(a, b, (((1,), (1,)), ((), ())), preferred_element_type=F32)

    def mm1(a, b):
        return _dot(a.astype(BF16), b.astype(BF16))

    mm_inv = mm1
    mm_aux = mm1
    mm_st = mm1

    refs = ((rf, vf, kf, lwf, kdf, bf), (rb, vb, kb, lwb, kdb, bb))
    units = [(d, (ci if d == 0 else n_ch - 1 - ci) * c, pr)
             for ci in range(n_ch) for d in range(2) for pr in range(RW_NP)]

    pre = {}
    for d, lo, pr in units:
        r_ref, v_ref, kk_ref, lw_ref, kd_ref, b_ref = refs[d]
        rows = slice(lo, lo + c)
        cols = slice(pr * LANES, (pr + 1) * LANES)
        strict = (si < ti) if d == 0 else (si > ti)
        incl = (si <= ti) if d == 0 else (si >= ti)
        ymask = incl if d == 0 else strict
        lw = lw_ref[rows, cols]
        tri = jnp.where(incl, 1.0, 0.0).astype(BF16)
        l1, l2, l3 = _split3(lw)
        logpi = _dot(tri, l1) + _dot(tri, l2) + _dot(tri, l3)
        pe = jnp.exp(logpi - lw)
        inv = jnp.exp(-logpi)
        qt = kk_ref[rows, cols] * pe
        kt = kd_ref[rows, cols] * inv
        bt = b_ref[rows, cols] * inv
        rt = r_ref[rows, cols] * (jnp.exp(logpi) if d == 0 else pe)
        v = v_ref[rows, cols]
        ktb, btb = kt.astype(BF16), bt.astype(BF16)
        heads = []
        for h in range(2):
            qh = (qt * hmask[h]).astype(BF16)
            rh = (rt * hmask[h]).astype(BF16)
            heads.append(dict(
                ak=jnp.where(strict, nt(qh, ktb), 0.0), ab=jnp.where(strict, nt(qh, btb), 0.0),
                mk=jnp.where(ymask, nt(rh, ktb), 0.0), mb=jnp.where(ymask, nt(rh, btb), 0.0),
                vh=v * hmask[h]))
        pre[(d, lo, pr)] = dict(qt=qt, rt=rt, heads=heads, ktv=mm1(kt.T, v), btt=bt.T,
                            ptot=jnp.exp(jnp.sum(lw.T, axis=1, keepdims=True)))

    inst = [hd for key in units for hd in pre[key]['heads']]
    for hd in inst:
        hd['pw'] = hd['ab']
        hd['x'] = eye - hd['ab']
    for _ in range(5):
        for hd in inst:
            hd['pw'] = mm_inv(hd['pw'], hd['pw'])
        for hd in inst:
            hd['x'] = hd['x'] + mm_inv(hd['x'], hd['pw'])
    eye_k = jnp.where(lax.broadcasted_iota(jnp.int32, (LANES, LANES), 0)
                      == lax.broadcasted_iota(jnp.int32, (LANES, LANES), 1), 1.0, 0.0)
    for key in units:
        pc = pre[key]
        wtq, ypre = None, None
        for h, hd in enumerate(pc['heads']):
            rhs = jnp.concatenate([mm_aux(hd['ak'], hd['vh']), pc['qt'] * hmask[h]], axis=1)
            t_rhs = mm_aux(hd['x'], rhs)
            yh = jnp.concatenate([mm_aux(hd['mk'], hd['vh']), pc['rt'] * hmask[h]], axis=1) \
                - mm_aux(hd['mb'], t_rhs)
            wtq = t_rhs if wtq is None else wtq + t_rhs
            ypre = yh if ypre is None else ypre + yh
        bw = mm_aux(pc['btt'], wtq)
        pc['ypre'] = ypre
        pc['cst'] = bdmask * (pc['ktv'] - bw[:, :LANES])
        pc['mtx'] = eye_k - bdmask * bw[:, LANES:]

    state = {(d, pr): st_scr[d, pr] for d in range(2) for pr in range(RW_NP)}
    for d, lo, pr in units:
        pc = pre[(d, lo, pr)]
        s0 = state[(d, pr)]
        y = pc['ypre'][:, :LANES] + mm_st(pc['ypre'][:, LANES:], s0)
        (yf_o if d == 0 else yb_o)[lo:lo + c, pr * LANES:(pr + 1) * LANES] = y
        state[(d, pr)] = pc['ptot'] * (_dot_x3(pc['mtx'], s0) + pc['cst'])
    for (d, pr), val in state.items():
        st_scr[d, pr] = val


def _rwkv_group_kernel(rf, vf, kf, lwf, kdf, bf, rb, vb, kb, lwb, kdb, bb, yf_o, yb_o, st_scr):
    @pl.when(pl.program_id(1) == 0)
    def _():
        st_scr[...] = jnp.zeros_like(st_scr)

    c = RW_C
    n_ch = RW_T2 // c
    g4 = 4 * c
    lane = lax.broadcasted_iota(jnp.int32, (1, LANES), 1)
    m0 = jnp.where(lane < RWKV_HEAD, 1.0, 0.0)
    m1 = jnp.where(lane >= RWKV_HEAD, 1.0, 0.0)
    bi = lax.broadcasted_iota(jnp.int32, (LANES, LANES), 0)
    bj = lax.broadcasted_iota(jnp.int32, (LANES, LANES), 1)
    bdmask = jnp.where(bi // RWKV_HEAD == bj // RWKV_HEAD, 1.0, 0.0)
    eye_k = jnp.where(bi == bj, 1.0, 0.0)
    tri_i = jnp.where(bi < c, jnp.where(bj <= bi, 1, 0), jnp.where(bj >= bi, 1, 0))
    tri_g = jnp.where((bi // c == bj // c) & (tri_i == 1), 1.0, 0.0).astype(BF16)
    is_fwd_row = lax.broadcasted_iota(jnp.int32, (2 * c, LANES), 0) < c

    ri = lax.broadcasted_iota(jnp.int32, (g4, g4), 0)
    ci_ = lax.broadcasted_iota(jnp.int32, (g4, g4), 1)
    same = ri // c == ci_ // c
    t_in, s_in = ri % c, ci_ % c
    fwd_blk = ri < 2 * c
    strict_i = jnp.where(fwd_blk, jnp.where(s_in < t_in, 1, 0), jnp.where(s_in > t_in, 1, 0))
    strict4 = same & (strict_i == 1)
    incl_f = same & fwd_blk & (s_in == t_in)
    ymask4 = strict4 | incl_f
    eye4 = jnp.where(ri == ci_, 1.0, 0.0)

    def nt(a, b):
        return lax.dot_general(a, b, (((1,), (1,)), ((), ())), preferred_element_type=F32)

    def mm1(a, b):
        return _dot(a.astype(BF16), b.astype(BF16))

    zero = jnp.zeros((c, LANES), F32)

    def x4(z):
        za, zb = z[:c], z[c:]
        return jnp.concatenate([
            jnp.concatenate([za * m0, zero], axis=1), jnp.concatenate([za * m1, zero], axis=1),
            jnp.concatenate([zero, zb * m0], axis=1), jnp.concatenate([zero, zb * m1], axis=1)], axis=0)

    def head_sum(x, lo):
        a = x[0:c] + x[c:2 * c]
        b = x[2 * c:3 * c] + x[3 * c:]
        return jnp.concatenate([
            jnp.concatenate([a[:, lo:lo + LANES], a[:, 2 * LANES + lo:3 * LANES + lo]], axis=1),
            jnp.concatenate([b[:, LANES + lo:2 * LANES + lo],
                             b[:, 3 * LANES + lo:4 * LANES + lo]], axis=1)], axis=0)

    refs_f = (rf, vf, kf, lwf, kdf, bf)
    refs_b = (rb, vb, kb, lwb, kdb, bb)
    groups = [(pr, ci) for ci in range(n_ch) for pr in range(RW_NP)]
    pre = {}
    for pr, ci in groups:
        cols = slice(pr * LANES, (pr + 1) * LANES)
        ra = slice(ci * c, (ci + 1) * c)
        rb_ = slice((n_ch - 1 - ci) * c, (n_ch - ci) * c)

        def both(idx):
            return jnp.concatenate([refs_f[idx][ra, cols], refs_b[idx][rb_, cols]], axis=0)

        r, v, kk, lw, kd, b = (both(i) for i in range(6))
        l1, l2, l3 = _split3(lw)
        logpi = _dot(tri_g, l1) + _dot(tri_g, l2) + _dot(tri_g, l3)
        pe = jnp.exp(logpi - lw)
        inv = jnp.exp(-logpi)
        qs, ks, bs = x4(kk * pe), x4(kd * inv), x4(b * inv)
        rs = x4(r * jnp.where(is_fwd_row, jnp.exp(logpi), pe))
        vs = x4(v)
        qb, kb_, bb_, rb2 = qs.astype(BF16), ks.astype(BF16), bs.astype(BF16), rs.astype(BF16)
        pre[(pr, ci)] = dict(
            qs=qs, rs=rs, vs=vs, v=v, kt=kd * inv, bt=b * inv,
            ak=jnp.where(strict4, nt(qb, kb_), 0.0), ab=jnp.where(strict4, nt(qb, bb_), 0.0),
            mk=jnp.where(ymask4, nt(rb2, kb_), 0.0), mb=jnp.where(ymask4, nt(rb2, bb_), 0.0),
            ptot=[jnp.exp(jnp.sum(lw[:c].T, axis=1, keepdims=True)),
                  jnp.exp(jnp.sum(lw[c:].T, axis=1, keepdims=True))])

    for key in groups:
        pre[key]['pw'] = pre[key]['ab']
        pre[key]['x'] = eye4 - pre[key]['ab']
    for _ in range(5):
        for key in groups:
            pre[key]['pw'] = mm1(pre[key]['pw'], pre[key]['pw'])
        for key in groups:
            pre[key]['x'] = pre[key]['x'] + mm1(pre[key]['x'], pre[key]['pw'])

    zero2 = jnp.zeros((c, 2 * LANES), F32)
    for key in groups:
        pc = pre[key]
        rhs = jnp.concatenate([mm1(pc['ak'], pc['vs']), pc['qs']], axis=1)
        t_rhs = mm1(pc['x'], rhs)
        yh = jnp.concatenate([mm1(pc['mk'], pc['vs']), pc['rs']], axis=1) - mm1(pc['mb'], t_rhs)
        wtq = head_sum(t_rhs, 0)
        pc['ypre'] = head_sum(yh, 0)
        kts = jnp.concatenate([jnp.concatenate([pc['kt'][:c], zero], axis=1),
                               jnp.concatenate([zero, pc['kt'][c:]], axis=1)], axis=0)
        bts = jnp.concatenate([jnp.concatenate([pc['bt'][:c], zero], axis=1),
                               jnp.concatenate([zero, pc['bt'][c:]], axis=1)], axis=0)
        ktv = mm1(kts.T, pc['v'])
        bw = mm1(bts.T, wtq)
        pc['cst'] = [bdmask * (ktv[u * LANES:(u + 1) * LANES] - bw[u * LANES:(u + 1) * LANES, :LANES])
                     for u in range(2)]
        pc['mtx'] = [eye_k - bdmask * bw[u * LANES:(u + 1) * LANES, LANES:] for u in range(2)]
    del zero2

    state = {(d, pr): st_scr[d, pr] for d in range(2) for pr in range(RW_NP)}
    for pr, ci in groups:
        pc = pre[(pr, ci)]
        for d in range(2):
            s0 = state[(d, pr)]
            yp = pc['ypre'][d * c:(d + 1) * c]
            y = yp[:, :LANES] + mm1(yp[:, LANES:], s0)
            lo = (ci if d == 0 else n_ch - 1 - ci) * c
            (yf_o if d == 0 else yb_o)[lo:lo + c, pr * LANES:(pr + 1) * LANES] = y
            state[(d, pr)] = pc['ptot'][d] * (_dot_x3(pc['mtx'][d], s0) + pc['cst'][d])
    for (d, pr), val in state.items():
        st_scr[d, pr] = val


def _rwkv_chunk_kernel_old(rf, vf, kf, lwf, kdf, bf, rb, vb, kb, lwb, kdb, bb, yf_o, yb_o, st_scr):
    @pl.when(pl.program_id(1) == 0)
    def _():
        st_scr[...] = jnp.zeros_like(st_scr)

    c = RW_C
    ti = lax.broadcasted_iota(jnp.int32, (c, c), 0)
    si = lax.broadcasted_iota(jnp.int32, (c, c), 1)
    lane = lax.broadcasted_iota(jnp.int32, (1, LANES), 1)
    hmask = [jnp.where(lane < RWKV_HEAD, 1.0, 0.0), jnp.where(lane >= RWKV_HEAD, 1.0, 0.0)]
    bi = lax.broadcasted_iota(jnp.int32, (LANES, LANES), 0) // RWKV_HEAD
    bj = lax.broadcasted_iota(jnp.int32, (LANES, LANES), 1) // RWKV_HEAD
    bdmask = jnp.where(bi == bj, 1.0, 0.0)
    eye = jnp.where(ti == si, 1.0, 0.0)

    def nt(a, b):
        return lax.dot_general(a, b, (((1,), (1,)), ((), ())), preferred_element_type=F32)

    def mm1(a, b):
        return _dot(a.astype(BF16), b.astype(BF16))

    def chunk(d, refs, lo, y_o):
        r_ref, v_ref, kk_ref, lw_ref, kd_ref, b_ref = refs
        rows = slice(lo, lo + c)
        strict = (si < ti) if d == 0 else (si > ti)
        incl = (si <= ti) if d == 0 else (si >= ti)
        lw = lw_ref[rows, :]
        r, v, kk, kd, b = r_ref[rows, :], v_ref[rows, :], kk_ref[rows, :], kd_ref[rows, :], b_ref[rows, :]

        tri = jnp.where(incl, 1.0, 0.0).astype(BF16)
        l1 = lw.astype(BF16)
        l2 = (lw - l1.astype(F32)).astype(BF16)
        l3 = (lw - l1.astype(F32) - l2.astype(F32)).astype(BF16)
        logpi = _dot(tri, l1) + _dot(tri, l2) + _dot(tri, l3)
        logpe = logpi - lw
        pe = jnp.exp(logpe)
        inv = jnp.exp(-logpi)
        qt = kk * pe
        kt = kd * inv
        bt = b * inv
        rt = r * (jnp.exp(logpi) if d == 0 else pe)
        tot = jnp.sum(lw.T, axis=1, keepdims=True)

        s0 = st_scr[d]
        ymask = incl if d == 0 else strict
        rhs = _dot_x3(qt, s0)
        y = _dot_x3(rt, s0)
        qb, ktb, btb, rb_ = qt.astype(BF16), kt.astype(BF16), bt.astype(BF16), rt.astype(BF16)
        u = jnp.zeros((c, LANES), F32)
        for h in range(2):
            m = hmask[h]
            qh = (qt * m).astype(BF16)
            rh = (rt * m).astype(BF16)
            ak = jnp.where(strict, nt(qh, ktb), 0.0)
            ab = jnp.where(strict, nt(qh, btb), 0.0)
            mk = jnp.where(ymask, nt(rh, ktb), 0.0)
            mb = jnp.where(ymask, nt(rh, btb), 0.0)
            vh = v * m
            g = (rhs * m) + _dot_x3(ak, vh)
            a2 = _dot_x3(ab, ab)
            a4 = _dot_x3(a2, a2)
            a8 = _dot_x3(a4, a4)
            a16 = _dot_x3(a8, a8)
            a32 = _dot_x3(a16, a16)
            for ap in (a32, a16, a8, a4, a2):
                g = g + _dot_x3(ap, g)
            uh = g - _dot_x3(ab, g)
            u = u + uh
            y = y + _dot_x3(mk, vh) - _dot_x3(mb, uh)
        y_o[rows, :] = y
        upd = mm1(kt.T, v) - mm1(bt.T, u)
        st_scr[d] = jnp.exp(tot) * (s0 + bdmask * upd)

    n_ch = RW_T2 // c
    fwd = (rf, vf, kf, lwf, kdf, bf)
    bwd = (rb, vb, kb, lwb, kdb, bb)
    for ci in range(n_ch):
        chunk(0, fwd, ci * c, yf_o)
        chunk(1, bwd, (n_ch - 1 - ci) * c, yb_o)


def _rwkv_chunks(r, v, kk, lw, kd, b):
    n_t = N_TOK // RW_T2
    ft = lambda s: s
    bt = lambda s: _bwd_tile(s, CTX_LEN // RW_T2, n_t)

    bw = RW_NP * LANES

    def one(f):
        return pl.BlockSpec((RW_T2, bw), lambda p, s: (f(s), p))

    def two(f, d):
        return pl.BlockSpec((None, RW_T2, bw), lambda p, s: (d, f(s), p))

    out = jax.ShapeDtypeStruct((N_TOK, GROUP_W), F32)
    return pl.pallas_call(
        _rwkv_group_kernel,
        out_shape=(out, out),
        grid=(GROUP_W // bw, n_t),
        in_specs=[one(ft), one(ft), one(ft), two(ft, 0), two(ft, 0), two(ft, 0),
                  one(bt), one(bt), one(bt), two(bt, 1), two(bt, 1), two(bt, 1)],
        out_specs=(one(ft), one(bt)),
        scratch_shapes=[pltpu.VMEM((2, RW_NP, LANES, LANES), F32)],
        compiler_params=_params(2), name="rwkv_chunks",
    )(r, v, kk, lw, kd, b, r, v, kk, lw, kd, b)


RW_BW = 256


def _rwkv_scan(r, v, kk, w, kd, b):
    i64 = jnp.arange(RWKV_HEAD)
    lw = jnp.arange(RW_BW)
    eye2 = (i64[:, None] == (lw % RWKV_HEAD)[None, :]).astype(F32)
    bd = ((lw // RWKV_HEAD)[:, None] == (lw // RWKV_HEAD)[None, :]).astype(BF16)

    ft = lambda s: s
    bt = lambda s: _bwd_tile(s, RW_CTX_TILES, RW_TILES)

    def one(f):
        return pl.BlockSpec((RW_TB, GROUP_W), lambda s: (f(s), 0))

    def two(f, d):
        return pl.BlockSpec((None, RW_TB, GROUP_W), lambda s: (d, f(s), 0))

    out = jax.ShapeDtypeStruct((N_TOK, GROUP_W), F32)
    return pl.pallas_call(
        _rwkv_scan_kernel,
        out_shape=(out, out),
        grid=(RW_TILES,),
        in_specs=[one(ft), one(ft), one(ft), two(ft, 0), two(ft, 0), two(ft, 0),
                  one(bt), one(bt), one(bt), two(bt, 1), two(bt, 1), two(bt, 1),
                  pl.BlockSpec((RWKV_HEAD, RW_BW), lambda s: (0, 0)),
                  pl.BlockSpec((RW_BW, RW_BW), lambda s: (0, 0))],
        out_specs=(one(ft), one(bt)),
        scratch_shapes=[pltpu.VMEM((2, RWKV_HEAD, GROUP_W), F32),
                        pltpu.VMEM((2, RW_TB, RWKV_HEAD, GROUP_W), F32),
                        pltpu.VMEM((2, RW_TB, RWKV_HEAD, GROUP_W), BF16)],
        compiler_params=_params(), name="rwkv_scan",
    )(r, v, kk, w, kd, b, r, v, kk, w, kd, b, eye2, bd)


def _merge_kernel(ya_ref, yb_ref, yc_ref, ydf_ref, ydb_ref, bonus_ref, g_ref, lnw_ref, lnb_ref,
                  ones_ref, gain_ref, wout_ref, xs_ref, mod_ref, n2_ref, rw_ref, rb_ref,
                  xo_ref, fx_ref, te_ref, tg_ref):
    i = pl.program_id(0)
    is_ctx = _row_is_ctx(i, TILE)
    ones = ones_ref[...]
    inv = 1.0 / RWKV_HEAD
    yd = ydf_ref[...] + ydb_ref[...]
    mean = _dot_sel(yd, ones) * inv
    dl = yd - mean
    var = _dot_sel(dl * dl, ones) * inv
    yd = (dl * lax.rsqrt(var + RWKV_LN_EPS) * lnw_ref[...] + lnb_ref[...] + bonus_ref[...]) * g_ref[...]

    parts = []
    for gi, y in enumerate((ya_ref[...], yb_ref[...], yc_ref[...], yd)):
        parts.append((_rms_rows(y) * gain_ref[:, gi * GROUP_W:(gi + 1) * GROUP_W]).astype(BF16))
    m = _dot(jnp.concatenate(parts, axis=1), wout_ref[...])
    xs = xs_ref[...] + _mod_rows(mod_ref, 2, is_ctx) * m
    xo_ref[...] = xs

    fx = _rms_rows(xs) * n2_ref[...]
    fx = fx * (1.0 + _mod_rows(mod_ref, 4, is_ctx)) + _mod_rows(mod_ref, 3, is_ctx)
    fx_ref[...] = fx

    logits = _dot_x3(fx, rw_ref[...]) + rb_ref[...]
    lane = lax.broadcasted_iota(jnp.int32, logits.shape, 1)
    vals = logits
    tops, idxs = [], []
    for _ in range(TOP_K):
        mx = jnp.max(vals, axis=-1, keepdims=True)
        ix = jnp.min(jnp.where(vals == mx, lane, LANES), axis=-1, keepdims=True)
        tops.append(mx)
        idxs.append(ix)
        vals = jnp.where(lane == ix, -jnp.inf, vals)
    es = [jnp.exp(t - tops[0]) for t in tops]
    den = es[0] + es[1] + es[2] + es[3]
    te = jnp.zeros(logits.shape, jnp.int32)
    tg = jnp.zeros(logits.shape, F32)
    for kx in range(TOP_K):
        te = jnp.where(lane == kx, idxs[kx], te)
        tg = jnp.where(lane == kx, es[kx] / den, tg)
    te_ref[...] = te
    tg_ref[...] = tg


def _merge(ya, yb, yc, ydf, ydb, bonus, g, ln_w, ln_b, gain, w_out_b, xs, mod_l, norm2,
           router_w, router_b):
    def const(shape):
        return pl.BlockSpec(shape, lambda i: (0,) * len(shape))

    grp = pl.BlockSpec((TILE, GROUP_W), lambda i: (i, 0))
    full = pl.BlockSpec((TILE, D_MODEL), lambda i: (i, 0))
    lanes = pl.BlockSpec((TILE, LANES), lambda i: (i, 0))
    rw = jnp.pad(router_w.astype(F32), ((0, 0), (0, LANES - N_EXPERTS)))
    rb = jnp.concatenate([router_b.astype(F32), jnp.full((LANES - N_EXPERTS,), -1e30, F32)])
    return pl.pallas_call(
        _merge_kernel,
        out_shape=(jax.ShapeDtypeStruct((N_TOK, D_MODEL), F32),
                   jax.ShapeDtypeStruct((N_TOK, D_MODEL), F32),
                   jax.ShapeDtypeStruct((N_TOK, LANES), jnp.int32),
                   jax.ShapeDtypeStruct((N_TOK, LANES), F32)),
        grid=(N_TILES,),
        in_specs=[grp] * 7 + [const((1, GROUP_W)), const((1, GROUP_W)), const((GROUP_W, GROUP_W)),
                              const((1, D_MODEL)), const((D_MODEL, D_MODEL)), full,
                              const((SUBLANES, 6 * D_MODEL)), const((1, D_MODEL)),
                              const((D_MODEL, LANES)), const((1, LANES))],
        out_specs=(full, full, lanes, lanes),
        compiler_params=_params(), name="merge",
    )(ya, yb, yc, ydf, ydb, bonus, g, ln_w.reshape(1, GROUP_W), ln_b.reshape(1, GROUP_W),
      _head_ones(), gain.reshape(1, D_MODEL), w_out_b, xs, mod_l, norm2.reshape(1, D_MODEL),
      rw, rb.reshape(1, LANES))


N_SLOTS = N_TOK * TOP_K
MOE_NBLK = N_SLOTS // MOE_BLOCK + N_EXPERTS
MOE_ROWS = MOE_NBLK * MOE_BLOCK
MOE_OUT_ROWS = N_SLOTS + 2 * MOE_BLOCK


ROUTE_PARTS = 2


def _route_kernel(part, e_ref, cur0_ref, tok0_hbm, dst0_hbm, tok_ref, dst_ref, cur_ref, sem):
    copies = [pltpu.make_async_copy(tok0_hbm, tok_ref, sem.at[0]),
              pltpu.make_async_copy(dst0_hbm, dst_ref, sem.at[1])]
    for cp in copies:
        cp.start()
    for cp in copies:
        cp.wait()
    for e in range(N_EXPERTS):
        cur_ref[e] = cur0_ref[e]

    def place(t, c):
        es = [e_ref[t * TOP_K + k] for k in range(TOP_K)]
        ps = [cur_ref[e] for e in es]
        for k in range(TOP_K):
            cur_ref[es[k]] = ps[k] + 1
        for k in range(TOP_K):
            tok_ref[ps[k]] = t
            dst_ref[ps[k] + MOE_BLOCK] = k * N_TOK + t
        return c
    n = N_TOK // ROUTE_PARTS
    lax.fori_loop(part * n, (part + 1) * n, place, 0)


def _route(top_e):
    assert MOE_BLOCK == 256 and TOP_K == 4
    flat_e = top_e.reshape(N_SLOTS)
    counts = jnp.sum((flat_e[:, None] == jnp.arange(N_EXPERTS, dtype=jnp.int32)[None, :])
                     .astype(jnp.int32), axis=0)
    padded = (counts + MOE_BLOCK - 1) // MOE_BLOCK * MOE_BLOCK
    ex = jnp.arange(N_EXPERTS)
    pends = jnp.sum(jnp.where(ex[None, :] <= ex[:, None], padded[None, :], 0), axis=1)
    smem = pl.BlockSpec(memory_space=pltpu.SMEM)
    row = jnp.arange(MOE_ROWS + MOE_BLOCK, dtype=jnp.int32) - MOE_BLOCK
    dump = N_SLOTS + ((row // MOE_BLOCK) % 2) * MOE_BLOCK + row % MOE_BLOCK
    cur = (pends - padded).astype(jnp.int32)
    rows_tok, rows_dst = jnp.zeros((MOE_ROWS,), jnp.int32), dump
    for part in range(ROUTE_PARTS):
        rows_tok, rows_dst, cur = pl.pallas_call(
            functools.partial(_route_kernel, part),
            out_shape=(jax.ShapeDtypeStruct((MOE_ROWS,), jnp.int32),
                       jax.ShapeDtypeStruct((MOE_ROWS + MOE_BLOCK,), jnp.int32),
                       jax.ShapeDtypeStruct((N_EXPERTS,), jnp.int32)),
            in_specs=[smem, smem, pl.BlockSpec(memory_space=pl.ANY), pl.BlockSpec(memory_space=pl.ANY)],
            out_specs=(smem, smem, smem),
            scratch_shapes=[pltpu.SemaphoreType.DMA((2,))],
            name="route",
        )(flat_e, cur, rows_tok, rows_dst)
    blk_row0 = jnp.arange(MOE_NBLK, dtype=jnp.int32) * MOE_BLOCK
    block_e = jnp.minimum(jnp.sum((pends[None, :] <= blk_row0[:, None]).astype(jnp.int32), axis=1),
                          N_EXPERTS - 1)
    n_used = (pends[-1] // MOE_BLOCK).astype(jnp.int32).reshape(1)
    return block_e, n_used, rows_tok, rows_dst


def _moe_kernel(be_ref, nu_ref, tok_ref, dst_ref, fx_hbm, w1_ref, b1_ref, w2_ref,
                b2_ref, y_hbm, xg_scr, yb_scr, gsem, ssem):
    i = pl.program_id(0)
    n_used = nu_ref[0]
    slot = i % 2

    def gather_start(blk, sl, r):
        tok = tok_ref[blk * MOE_BLOCK + r]
        pltpu.make_async_copy(fx_hbm.at[pl.ds(tok, 1), :],
                              xg_scr.at[sl, pl.ds(r, 1), :], gsem.at[sl]).start()

    def scatter_start(blk, sl, r):
        dst = dst_ref[(blk + 1) * MOE_BLOCK + r]
        pltpu.make_async_copy(yb_scr.at[sl, pl.ds(r, 1), :],
                              y_hbm.at[pl.ds(dst, 1), :], ssem.at[sl]).start()

    def block_copy(sl, sem):
        return pltpu.make_async_copy(yb_scr.at[sl], y_hbm.at[pl.ds(N_SLOTS, MOE_BLOCK), :],
                                     sem.at[sl])

    def for_rows(fn):
        def body(r, c):
            fn(r)
            return c
        lax.fori_loop(0, MOE_BLOCK, body, 0)

    @pl.when(i == 0)
    def _():
        yb_scr[...] = jnp.zeros_like(yb_scr)
        block_copy(0, ssem).start()
        for_rows(lambda r: gather_start(0, 0, r))

    @pl.when(i < n_used)
    def _():
        block_copy(slot, gsem).wait()
        x = xg_scr[slot].astype(BF16)
        for r in range(MOE_BLOCK):
            gather_start(i + 1, 1 - slot, r)
        for r in range(MOE_BLOCK):
            scatter_start(i - 1, 1 - slot, r)
        hid = _dot(x, w1_ref[...]) + b1_ref[...]
        x_glu = jnp.minimum(hid[:, :D_FF], SWIGLU_LIMIT)
        x_lin = jnp.clip(hid[:, D_FF:], -SWIGLU_LIMIT, SWIGLU_LIMIT)
        act = x_glu * jax.nn.sigmoid(SWIGLU_ALPHA * x_glu) * (x_lin + 1.0)
        y = _dot(act.astype(BF16), w2_ref[...].astype(BF16)) + b2_ref[...]
        block_copy(slot, ssem).wait()
        yb_scr[slot] = y

    @pl.when(i == n_used)
    def _():
        block_copy(slot, gsem).wait()
        block_copy(slot, ssem).wait()
        for_rows(lambda r: scatter_start(i - 1, 1 - slot, r))
        block_copy(1 - slot, ssem).wait()


def _moe(fx, route, l, w1_b, b1, w2_b, b2):
    block_e, n_used, rows_tok, rows_dst = route
    grid_spec = pltpu.PrefetchScalarGridSpec(
        num_scalar_prefetch=4,
        grid=(MOE_NBLK,),
        in_specs=[
            pl.BlockSpec(memory_space=pl.ANY),
            pl.BlockSpec((None, None, D_MODEL, 2 * D_FF), lambda i, be, *_: (l, be[i], 0, 0)),
            pl.BlockSpec((None, None, 1, 2 * D_FF), lambda i, be, *_: (l, be[i], 0, 0)),
            pl.BlockSpec((None, None, D_FF, D_MODEL), lambda i, be, *_: (l, be[i], 0, 0)),
            pl.BlockSpec((None, None, 1, D_MODEL), lambda i, be, *_: (l, be[i], 0, 0)),
        ],
        out_specs=pl.BlockSpec(memory_space=pl.ANY),
        scratch_shapes=[pltpu.VMEM((2, MOE_BLOCK, D_MODEL), F32),
                        pltpu.VMEM((2, MOE_BLOCK, D_MODEL), F32),
                        pltpu.SemaphoreType.DMA((2,)),
                        pltpu.SemaphoreType.DMA((2,))],
    )
    return pl.pallas_call(
        _moe_kernel,
        out_shape=jax.ShapeDtypeStruct((MOE_OUT_ROWS, D_MODEL), F32),
        grid_spec=grid_spec,
        compiler_params=_params(), name="moe",
    )(block_e, n_used, rows_tok, rows_dst, fx, w1_b,
      b1.reshape(DEPTH, N_EXPERTS, 1, 2 * D_FF), w2_b, b2.reshape(DEPTH, N_EXPERTS, 1, D_MODEL))


def _combine_kernel(final, tile0, y0_ref, y1_ref, y2_ref, y3_ref, tg_ref, xs_ref, mod_ref, fn_ref,
                    o_ref):
    is_ctx = _row_is_ctx(pl.program_id(0) + tile0, TILE)
    f = None
    for k, y_ref in enumerate((y0_ref, y1_ref, y2_ref, y3_ref)):
        yk = y_ref[...] * tg_ref[:, k:k + 1]
        f = yk if f is None else f + yk
    xs = xs_ref[...] + _mod_rows(mod_ref, 5, is_ctx) * f
    if final:
        xs = _rms_rows(xs) * fn_ref[...]
    o_ref[...] = xs


def _combine(y4, top_g, xs, mod_l, final_norm, final):
    tile0 = CTX_LEN // TILE if final else 0
    n_out = SEQ if final else N_TOK
    return pl.pallas_call(
        functools.partial(_combine_kernel, final, tile0),
        out_shape=jax.ShapeDtypeStruct((n_out, D_MODEL), F32),
        grid=(n_out // TILE,),
        in_specs=[pl.BlockSpec((TILE, D_MODEL), functools.partial(lambda k, i: (k * N_TILES + i + tile0, 0), k))
                  for k in range(TOP_K)] + [
                  pl.BlockSpec((TILE, LANES), lambda i: (i + tile0, 0)),
                  pl.BlockSpec((TILE, D_MODEL), lambda i: (i + tile0, 0)),
                  pl.BlockSpec((SUBLANES, 6 * D_MODEL), lambda i: (0, 0)),
                  pl.BlockSpec((1, D_MODEL), lambda i: (0, 0))],
        out_specs=pl.BlockSpec((TILE, D_MODEL), lambda i: (i, 0)),
        compiler_params=_params(), name="combine_final" if final else "combine",
    )(y4, y4, y4, y4, top_g, xs, mod_l, final_norm.reshape(1, D_MODEL))


def _w_in_layout(w):
    g = GROUP_W
    s5, ret, lru, rw = w[:, :g], w[:, g:5 * g], w[:, 5 * g:7 * g], w[:, 7 * g:]
    z = jnp.zeros((D_MODEL, LANES - RWKV_RANK), w.dtype)
    rw = jnp.concatenate([rw[:, :3 * g], rw[:, 3 * g:3 * g + RWKV_RANK], z,
                          rw[:, 3 * g + RWKV_RANK:3 * g + 2 * RWKV_RANK], z,
                          rw[:, 3 * g + 2 * RWKV_RANK:]], axis=1)
    return jnp.concatenate([rw, s5, ret, lru], axis=1).astype(BF16)


def kernel(x, c, ctx, c_ctx, w_ada, b_ada, norm1, norm2, w_in, w_out, mix_gain, s5_lam_re, s5_lam_im, s5_log_dt, s5_b_re, s5_b_im, s5_c_re, s5_c_im, s5_d, s5_w_glu, s5_b_glu, ret_decay, lru_conv_w, lru_conv_b, lru_lam, lru_w_r, lru_b_r, lru_w_i, lru_b_i, rwkv_mu, rwkv_w0, rwkv_w_up, rwkv_a0, rwkv_a_up, rwkv_g_up, rwkv_k_k, rwkv_k_a, rwkv_r_k, rwkv_ln_w, rwkv_ln_b, router_w, router_b, exp_w1, exp_b1, exp_w2, exp_b2, final_norm):
    assert x.shape == (1, SEQ, D_MODEL) and ctx.shape == (1, CTX_LEN, D_MODEL)
    xs = jnp.concatenate([ctx[0], x[0]], axis=0).astype(F32)
    cc = jnp.zeros((SUBLANES, D_MODEL), F32).at[0].set(c[0]).at[1].set(c_ctx)
    mods = _modulation(cc, w_ada, b_ada)
    rope = _rope_tables()
    w1_b = exp_w1.astype(BF16)

    for l in range(DEPTH):
        mod_l = mods[l]
        proj = _inproj(xs, norm1[l], mod_l, _w_in_layout(w_in[l]))

        s5_tabs = [_s5_tables(s5_lam_re[l, d], s5_lam_im[l, d], s5_log_dt[l, d], s5_b_re[l, d],
                              s5_b_im[l, d], s5_c_re[l, d], s5_c_im[l, d], d == 1) for d in range(2)]
        ya = _s5_mixer(proj, s5_tabs[0], s5_tabs[1], s5_d[l], s5_w_glu[l], s5_b_glu[l])

        ret_f, ret_b = _ret_tables(ret_decay[l])
        yb = _ret_mixer(proj, rope, ret_f, ret_b)

        yc = _lru_mixer(proj, lru_conv_w[l], lru_conv_b[l], lru_lam[l], lru_w_r[l], lru_b_r[l],
                        lru_w_i[l], lru_b_i[l])

        r, v, kk, bonus, g, w, kd, b = _rwkv_prep(
            proj, rwkv_mu[l], rwkv_k_k[l], rwkv_k_a[l], rwkv_r_k[l], rwkv_w0[l], rwkv_w_up[l],
            rwkv_a0[l], rwkv_a_up[l], rwkv_g_up[l])
        ydf, ydb = _rwkv_chunks(r, v, kk, w, kd, b)

        xs, fx, top_e, top_g = _merge(ya, yb, yc, ydf, ydb, bonus, g, rwkv_ln_w[l], rwkv_ln_b[l],
                                      mix_gain[l], w_out[l].astype(BF16), xs, mod_l, norm2[l],
                                      router_w[l], router_b[l])
        route = _route(top_e[:, :TOP_K])
        y4 = _moe(fx, route, l, w1_b, exp_b1, exp_w2, exp_b2)
        xs = _combine(y4, top_g, xs, mod_l, final_norm, l == DEPTH - 1)

    return xs.reshape(1, SEQ, D_MODEL)
```

```python
import functools
import math

import jax
import jax.numpy as jnp
from jax import lax
from jax.experimental import pallas as pl
from jax.experimental.pallas import tpu as pltpu

F32 = jnp.float32
BF16 = jnp.bfloat16

D_MODEL = 2048
SEQ = 8192
CTX_LEN = 256
N_TOK = SEQ + CTX_LEN
DEPTH = 4
GRID_W = 64
GROUP_W = 512
NORM_EPS = 1e-6
S5_CH = 16
S5_GROUPS = 32
S5_STATE = 64
RET_HEADS = 4
RET_DK = 128
RET_CHUNK = 128
ROPE_BASE = 10000.0
LRU_BLOCKS = 8
LRU_BW = 64
LRU_CONV = 4
LRU_C = 8.0
RWKV_HEAD = 64
RWKV_HEADS = 8
RWKV_RANK = 96
RWKV_GATE_RANK = 256
RWKV_LN_EPS = 64e-5
N_EXPERTS = 32
TOP_K = 4
D_FF = 896
SWIGLU_ALPHA = 1.702
SWIGLU_LIMIT = 7.0
MOE_BLOCK = 256

LANES = 128
SUBLANES = 8
VMEM_LIMIT = 56 * 1024 * 1024

N_PROJ = 5632
COL_RWKV = 0
COL_S5 = 2048
COL_RET = 2560
COL_LRU = 4608

TILE = 256
N_TILES = N_TOK // TILE
S5_SEG = 32
RET_TILES = N_TOK // RET_CHUNK
RET_CTX_TILES = CTX_LEN // RET_CHUNK


def _params(n_axes=1):
    return pltpu.CompilerParams(
        dimension_semantics=("arbitrary",) * n_axes, vmem_limit_bytes=VMEM_LIMIT)


def _dot(a, b):
    return jnp.dot(a, b, preferred_element_type=F32)


def _split2(x):
    hi = x.astype(BF16)
    lo = (x - hi.astype(F32)).astype(BF16)
    return hi, lo


def _dot_x3(a, b):
    ah, al = _split2(a)
    bh, bl = _split2(b)
    return _dot(ah, bh) + _dot(ah, bl) + _dot(al, bh)


def _dot_sel(a, sel):
    ah, al = _split2(a)
    return _dot(ah, sel) + _dot(al, sel)


def _sel_dot(sel, a):
    ah, al = _split2(a)
    return _dot(sel, ah) + _dot(sel, al)


def _bwd_tile(s, n_ctx, n_all):
    return jnp.where(s < n_ctx, n_ctx - 1 - s, n_all - 1 - (s - n_ctx))


def _rms_rows(x):
    return x * lax.rsqrt(jnp.mean(x * x, axis=-1, keepdims=True) + NORM_EPS)


def _row_is_ctx(tile_idx, tile_rows):
    row = tile_idx * tile_rows + lax.broadcasted_iota(jnp.int32, (tile_rows, 1), 0)
    return row < CTX_LEN


def _mod_kernel(cc_ref, w_ref, b_ref, o_ref):
    s = cc_ref[...]
    s = s * jax.nn.sigmoid(s)
    o_ref[...] = _dot(s.astype(BF16), w_ref[...].astype(BF16)) + b_ref[...]


def _modulation(cc, w_ada, b_ada):
    tn = 1024
    nb = 6 * D_MODEL // tn
    return pl.pallas_call(
        _mod_kernel,
        out_shape=jax.ShapeDtypeStruct((DEPTH, SUBLANES, 6 * D_MODEL), F32),
        grid=(DEPTH, nb),
        in_specs=[
            pl.BlockSpec((SUBLANES, D_MODEL), lambda l, j: (0, 0)),
            pl.BlockSpec((None, D_MODEL, tn), lambda l, j: (l, 0, j)),
            pl.BlockSpec((None, 1, tn), lambda l, j: (l, 0, j)),
        ],
        out_specs=pl.BlockSpec((None, SUBLANES, tn), lambda l, j: (l, 0, j)),
        compiler_params=_params(2),
        name="modulation",
    )(cc, w_ada, b_ada.reshape(DEPTH, 1, 6 * D_MODEL))


def _mod_rows(mod_ref, chunk, is_ctx):
    lo, hi = chunk * D_MODEL, (chunk + 1) * D_MODEL
    return jnp.where(is_ctx, mod_ref[1:2, lo:hi], mod_ref[0:1, lo:hi])


IN_TM = 768
IN_TN = 512


def _inproj_kernel(x_ref, g_ref, mod_ref, w_ref, o_ref, h_scr):
    i, j = pl.program_id(0), pl.program_id(1)

    @pl.when(j == 0)
    def _():
        is_ctx = _row_is_ctx(i, IN_TM)
        xn = _rms_rows(x_ref[...]) * g_ref[...]
        h = xn * (1.0 + _mod_rows(mod_ref, 1, is_ctx)) + _mod_rows(mod_ref, 0, is_ctx)
        h_scr[...] = h.astype(BF16)

    o_ref[...] = _dot(h_scr[...], w_ref[...])


def _inproj(xs, norm_g, mod_l, w_in_b):
    return pl.pallas_call(
        _inproj_kernel,
        out_shape=jax.ShapeDtypeStruct((N_TOK, N_PROJ), F32),
        grid=(N_TOK // IN_TM, N_PROJ // IN_TN),
        in_specs=[
            pl.BlockSpec((IN_TM, D_MODEL), lambda i, j: (i, 0)),
            pl.BlockSpec((1, D_MODEL), lambda i, j: (0, 0)),
            pl.BlockSpec((SUBLANES, 6 * D_MODEL), lambda i, j: (0, 0)),
            pl.BlockSpec((D_MODEL, IN_TN), lambda i, j: (0, j)),
        ],
        out_specs=pl.BlockSpec((IN_TM, IN_TN), lambda i, j: (i, j)),
        scratch_shapes=[pltpu.VMEM((IN_TM, D_MODEL), BF16)],
        compiler_params=_params(2),
        name="inproj",
    )(xs, norm_g.reshape(1, D_MODEL), mod_l, w_in_b)


S5_NQ = 4
S5_QW = 2 * 8 * S5_STATE
S5_W = S5_NQ * S5_QW


def _s5_swap(h):
    half = S5_QW // 2
    parts = []
    for q in range(S5_NQ):
        parts.append(h[:, q * S5_QW + half:(q + 1) * S5_QW])
        parts.append(h[:, q * S5_QW:q * S5_QW + half])
    return jnp.concatenate(parts, axis=1)


def _s5_kernel(rev, *refs):
    if rev:
        (u_ref, yf_ref, perm_ref, permt_ref, bblk_ref, cblk_ref, a_ref, pw_ref,
         dskip_ref, wglu_ref, bglu_ref, o_ref, carry_scr, bu_scr, hl_scr, hs_scr) = refs
    else:
        (u_ref, perm_ref, permt_ref, bblk_ref, cblk_ref, a_ref, pw_ref,
         o_ref, carry_scr, bu_scr, hl_scr, hs_scr) = refs

    @pl.when(pl.program_id(0) == 0)
    def _():
        carry_scr[...] = jnp.zeros_like(carry_scr)

    u = u_ref[...]
    up = _dot(perm_ref[...], u.astype(BF16)).astype(BF16)
    for q in range(S5_NQ):
        bu = _dot(up[:, q * LANES:(q + 1) * LANES], bblk_ref[q])
        bu_scr[:, :, q * S5_QW:(q + 1) * S5_QW] = bu.reshape(S5_SEG, SUBLANES, S5_QW)

    a1 = jnp.broadcast_to(a_ref[0:1, :], (SUBLANES, S5_W))
    a2 = jnp.broadcast_to(a_ref[1:2, :], (SUBLANES, S5_W))
    at1, at2 = a_ref[2:3, :], a_ref[3:4, :]

    def step(s, h):
        p = S5_SEG - 1 - s if rev else s
        h = a1 * h + a2 * _s5_swap(h) + bu_scr[p]
        hl_scr[p] = h
        return h

    ends = lax.fori_loop(0, S5_SEG, step, jnp.zeros((SUBLANES, S5_W), F32))

    c = carry_scr[...]
    for j in (range(SUBLANES - 1, -1, -1) if rev else range(SUBLANES)):
        hs_scr[j:j + 1, :] = c
        c = ends[j:j + 1, :] + at1 * c + at2 * _s5_swap(c)
    carry_scr[...] = c

    hs = hs_scr[...]
    hsw = _s5_swap(hs)

    def fix(p8, carry):
        base = pl.multiple_of(p8 * SUBLANES, SUBLANES)
        pw1 = pw_ref[0, pl.ds(base, SUBLANES), :]
        pw2 = pw_ref[1, pl.ds(base, SUBLANES), :]
        for tt in range(SUBLANES):
            p = base + tt
            hl_scr[p] = hl_scr[p] + pw1[tt:tt + 1, :] * hs + pw2[tt:tt + 1, :] * hsw
        return carry

    lax.fori_loop(0, S5_SEG // SUBLANES, fix, 0)

    ys = []
    for q in range(S5_NQ):
        hq = hl_scr[:, :, q * S5_QW:(q + 1) * S5_QW].reshape(TILE, S5_QW)
        ys.append(_dot(hq.astype(BF16), cblk_ref[q]))
    y = _sel_dot(permt_ref[...], jnp.concatenate(ys, axis=1))

    if rev:
        y = y + yf_ref[...] + dskip_ref[...] * u
        y = jax.nn.gelu(y)
        gate = _dot(y.astype(BF16), wglu_ref[...]) + bglu_ref[...]
        o_ref[...] = y * jax.nn.sigmoid(gate)
    else:
        o_ref[...] = y


def _s5_tables(lam_re, lam_im, log_dt, b_re, b_im, c_re, c_im, rev):
    dt = jnp.exp(log_dt.astype(F32))[:, None]
    lr = jnp.minimum(lam_re.astype(F32), -1e-4)
    li = lam_im.astype(F32)

    def a_pow(k):
        mag = jnp.exp(k * lr * dt)
        return mag * jnp.cos(k * li * dt), mag * jnp.sin(k * li * dt)

    def cols(re, im):
        lead = re.shape[:-2]
        re = re.reshape(lead + (S5_NQ, 8 * S5_STATE))
        im = im.reshape(lead + (S5_NQ, 8 * S5_STATE))
        return jnp.concatenate([re, im], axis=-1).reshape(lead + (S5_W,))

    ab_re, ab_im = a_pow(1.0)
    den = lr * lr + li * li
    f_re = ((ab_re - 1.0) * lr + ab_im * li) / den
    f_im = (ab_im * lr - (ab_re - 1.0) * li) / den
    bb_re = f_re[..., None] * b_re - f_im[..., None] * b_im
    bb_im = f_re[..., None] * b_im + f_im[..., None] * b_re
    at_re, at_im = a_pow(float(S5_SEG))
    a_tab = jnp.stack([cols(ab_re, ab_re), cols(-ab_im, ab_im),
                       cols(at_re, at_re), cols(-at_im, at_im)])
    ks = jnp.arange(S5_SEG, dtype=F32)
    ks = (S5_SEG - ks) if rev else (ks + 1.0)
    pk_re, pk_im = a_pow(ks[:, None, None])
    pw = jnp.stack([cols(pk_re, pk_re), cols(-pk_im, pk_im)])

    eye = jnp.eye(8, dtype=F32)

    def bdiag(m):
        a, b = m.shape[1:]
        m4 = m.reshape(S5_NQ, 8, a, b)
        return jnp.einsum('qgab,gh->qgahb', m4, eye).reshape(S5_NQ, 8 * a, 8 * b)

    bblk = jnp.concatenate([bdiag(jnp.swapaxes(bb_re, 1, 2)),
                            bdiag(jnp.swapaxes(bb_im, 1, 2))], axis=2).astype(BF16)
    cblk = jnp.concatenate([bdiag(jnp.swapaxes(c_re.astype(F32), 1, 2)),
                            bdiag(-jnp.swapaxes(c_im.astype(F32), 1, 2))], axis=1).astype(BF16)
    return a_tab, pw, bblk, cblk


def _s5_perm():
    r = jnp.arange(TILE)
    src = (r % SUBLANES) * S5_SEG + r // SUBLANES
    perm = (src[:, None] == jnp.arange(TILE)[None, :]).astype(BF16)
    return perm, perm.T


def _s5_mixer(proj, tabs_f, tabs_b, d_skip, w_glu, b_glu):
    perm, permt = _s5_perm()
    col = COL_S5 // GROUP_W

    def const(shape):
        return pl.BlockSpec(shape, lambda s: (0,) * len(shape))

    tab_specs = [const((TILE, TILE)), const((TILE, TILE)),
                 const((S5_NQ, LANES, S5_QW)), const((S5_NQ, S5_QW, LANES)),
                 const((4, S5_W)), const((2, S5_SEG, S5_W))]
    scratch = [pltpu.VMEM((1, S5_W), F32),
               pltpu.VMEM((S5_SEG, SUBLANES, S5_W), F32),
               pltpu.VMEM((S5_SEG, SUBLANES, S5_W), F32),
               pltpu.VMEM((SUBLANES, S5_W), F32)]
    out_shape = jax.ShapeDtypeStruct((N_TOK, GROUP_W), F32)

    a_tab, pw, bblk, cblk = tabs_f
    yf = pl.pallas_call(
        functools.partial(_s5_kernel, False),
        out_shape=out_shape, grid=(N_TILES,),
        in_specs=[pl.BlockSpec((TILE, GROUP_W), lambda s: (s, col))] + tab_specs,
        out_specs=pl.BlockSpec((TILE, GROUP_W), lambda s: (s, 0)),
        scratch_shapes=scratch, compiler_params=_params(), name="s5_fwd",
    )(proj, perm, permt, bblk, cblk, a_tab, pw)

    a_tab, pw, bblk, cblk = tabs_b
    bt = lambda s: _bwd_tile(s, 1, N_TILES)
    return pl.pallas_call(
        functools.partial(_s5_kernel, True),
        out_shape=out_shape, grid=(N_TILES,),
        in_specs=[pl.BlockSpec((TILE, GROUP_W), lambda s: (bt(s), col)),
                  pl.BlockSpec((TILE, GROUP_W), lambda s: (bt(s), 0))] + tab_specs
                 + [const((1, GROUP_W)), const((GROUP_W, GROUP_W)), const((1, GROUP_W))],
        out_specs=pl.BlockSpec((TILE, GROUP_W), lambda s: (bt(s), 0)),
        scratch_shapes=scratch, compiler_params=_params(), name="s5_bwd",
    )(proj, yf, perm, permt, bblk, cblk, a_tab, pw,
      d_skip.reshape(1, GROUP_W), w_glu.astype(BF16), b_glu.reshape(1, GROUP_W))


def _rope_shuffle(x):
    lane = lax.broadcasted_iota(jnp.int32, x.shape, 1)
    return jnp.where(lane % 64 < 32, pltpu.roll(x, LANES - 32, 1), pltpu.roll(x, 32, 1))


def _ret_kernel(rev, *refs):
    if rev:
        (q_ref, k_ref, v_ref, g_ref, of_ref, cos_ref, sin_ref, dintra_ref, dq_ref, dk_ref,
         dc_ref, o_ref, s_scr) = refs
    else:
        (q_ref, k_ref, v_ref, cos_ref, sin_ref, dintra_ref, dq_ref, dk_ref,
         dc_ref, o_ref, s_scr) = refs

    @pl.when(pl.program_id(0) == 0)
    def _():
        s_scr[...] = jnp.zeros_like(s_scr)

    cos, sin = cos_ref[...], sin_ref[...]
    outs = []
    for h in range(RET_HEADS):
        sl = slice(h * RET_DK, (h + 1) * RET_DK)
        qh = q_ref[:, sl]
        kh = k_ref[:, sl] * (RET_DK ** -0.5)
        qh = qh * cos + _rope_shuffle(qh) * sin
        kh = kh * cos + _rope_shuffle(kh) * sin
        qb, kb, vb = qh.astype(BF16), kh.astype(BF16), v_ref[:, sl].astype(BF16)
        sc = lax.dot_general(qb, kb, (((1,), (1,)), ((), ())),
                             preferred_element_type=F32) * dintra_ref[h]
        s_old = s_scr[h]
        o = _dot(sc.astype(BF16), vb) + _dot(qb, s_old.astype(BF16)) * dq_ref[h]
        kt = (kh * dk_ref[h]).T.astype(BF16)
        s_scr[h] = s_old * dc_ref[h] + _dot(kt, vb)
        if rev:
            o = o + of_ref[:, sl]
            gh = g_ref[:, sl]
            o = o * lax.rsqrt(jnp.mean(o * o, axis=-1, keepdims=True) + NORM_EPS)
            o = o * (gh * jax.nn.sigmoid(gh))
        outs.append(o)
    o_ref[...] = jnp.concatenate(outs, axis=1)


def _ret_tables(decay):
    log_g = -jnp.exp(decay.astype(F32))
    idx = jnp.arange(RET_CHUNK, dtype=F32)
    rel = idx[:, None] - idx[None, :]
    c = float(RET_CHUNK)
    lf = log_g[0][:, None, None]
    lb = log_g[1][:, None, None]
    d_f = jnp.where((rel >= 0)[None], jnp.exp(jnp.maximum(rel, 0.0)[None] * lf), 0.0)
    d_b = jnp.where((rel < 0)[None], jnp.exp(jnp.maximum(-rel, 0.0)[None] * lb), 0.0)
    ones = jnp.ones((1, 1, RET_DK), F32)
    dq_f = jnp.exp((idx + 1.0)[None, :, None] * lf) * ones
    dk_f = jnp.exp((c - 1.0 - idx)[None, :, None] * lf) * ones
    dq_b = jnp.exp((c - idx)[None, :, None] * lb) * ones
    dk_b = jnp.exp(idx[None, :, None] * lb) * ones
    dc_f = jnp.exp(c * lf) * ones
    dc_b = jnp.exp(c * lb) * ones
    return (d_f, dq_f, dk_f, dc_f), (d_b, dq_b, dk_b, dc_b)


def _rope_tables():
    m = 32
    inv = ROPE_BASE ** (-jnp.arange(m, dtype=F32) / m)
    t = jnp.arange(SEQ)
    ang_r = (t // GRID_W).astype(F32)[:, None] * inv[None, :]
    ang_c = (t % GRID_W).astype(F32)[:, None] * inv[None, :]
    cos = jnp.concatenate([jnp.cos(ang_r)] * 2 + [jnp.cos(ang_c)] * 2, axis=1)
    sin = jnp.concatenate([-jnp.sin(ang_r), jnp.sin(ang_r), -jnp.sin(ang_c), jnp.sin(ang_c)], axis=1)
    cos = jnp.concatenate([jnp.ones((CTX_LEN, RET_DK), F32), cos], axis=0)
    sin = jnp.concatenate([jnp.zeros((CTX_LEN, RET_DK), F32), sin], axis=0)
    return cos, sin


def _ret_mixer(proj, rope, tabs_f, tabs_b):
    cos, sin = rope
    c0 = COL_RET // GROUP_W

    def const(shape):
        return pl.BlockSpec(shape, lambda s: (0,) * len(shape))

    def tab_specs():
        return [const((RET_HEADS, RET_CHUNK, RET_CHUNK)), const((RET_HEADS, RET_CHUNK, RET_DK)),
                const((RET_HEADS, RET_CHUNK, RET_DK)), const((RET_HEADS, 1, RET_DK))]

    out_shape = jax.ShapeDtypeStruct((N_TOK, GROUP_W), F32)
    scratch = [pltpu.VMEM((RET_HEADS, RET_DK, RET_DK), F32)]

    def tok(cb, f):
        return pl.BlockSpec((RET_CHUNK, GROUP_W), lambda s: (f(s), cb))

    def rope_spec(f):
        return pl.BlockSpec((RET_CHUNK, RET_DK), lambda s: (f(s), 0))

    ident = lambda s: s
    of = pl.pallas_call(
        functools.partial(_ret_kernel, False),
        out_shape=out_shape, grid=(RET_TILES,),
        in_specs=[tok(c0, ident), tok(c0 + 1, ident), tok(c0 + 2, ident),
                  rope_spec(ident), rope_spec(ident)] + tab_specs(),
        out_specs=tok(0, ident),
        scratch_shapes=scratch, compiler_params=_params(), name="ret_fwd",
    )(proj, proj, proj, cos, sin, *tabs_f)

    bt = lambda s: _bwd_tile(s, RET_CTX_TILES, RET_TILES)
    return pl.pallas_call(
        functools.partial(_ret_kernel, True),
        out_shape=out_shape, grid=(RET_TILES,),
        in_specs=[tok(c0, bt), tok(c0 + 1, bt), tok(c0 + 2, bt), tok(c0 + 3, bt), tok(0, bt),
                  rope_spec(bt), rope_spec(bt)] + tab_specs(),
        out_specs=tok(0, bt),
        scratch_shapes=scratch, compiler_params=_params(), name="ret_bwd",
    )(proj, proj, proj, proj, of, cos, sin, *tabs_b)


def _lru_kernel(rev, *refs):
    if rev:
        (pv_ref, x_ref, nx_ref, gate_ref, hf_ref, cw_ref, cb_ref, sp_ref, wr_ref, br_ref,
         wi_ref, bi_ref, o_ref, carry_scr) = refs
    else:
        (pv_ref, x_ref, nx_ref, cw_ref, cb_ref, sp_ref, wr_ref, br_ref,
         wi_ref, bi_ref, o_ref, carry_scr) = refs
    s = pl.program_id(0)
    tile = _bwd_tile(s, 1, N_TILES) if rev else s

    @pl.when(s == 0)
    def _():
        carry_scr[...] = jnp.zeros_like(carry_scr)

    has_prev = jnp.logical_and(tile != 0, tile != 1).astype(F32)
    has_next = jnp.logical_and(tile != 0, tile != N_TILES - 1).astype(F32)
    x = x_ref[...]
    pv = pv_ref[...] * has_prev
    nx = nx_ref[...] * has_next
    row = lax.broadcasted_iota(jnp.int32, (TILE, GROUP_W), 0)
    xm1 = jnp.where(row == 0, pv[7:8, :], pltpu.roll(x, 1, 0))
    xm2 = jnp.where(row == 0, pv[6:7, :], jnp.where(row == 1, pv[7:8, :], pltpu.roll(x, 2, 0)))
    xp1 = jnp.where(row == TILE - 1, nx[0:1, :], pltpu.roll(x, TILE - 1, 0))
    xc = (xm2 * cw_ref[0:1, :] + xm1 * cw_ref[1:2, :] + x * cw_ref[2:3, :]
          + xp1 * cw_ref[3:4, :] + cb_ref[...])

    r = jax.nn.sigmoid(_dot_x3(xc, wr_ref[...]) + br_ref[...])
    ig = jax.nn.sigmoid(_dot_x3(xc, wi_ref[...]) + bi_ref[...])
    log_a = -LRU_C * r * sp_ref[...]
    a = jnp.exp(log_a)
    b = jnp.sqrt(1.0 - jnp.exp(2.0 * log_a)) * (ig * xc)

    sh = 1
    while sh < TILE:
        if rev:
            keep = row < TILE - sh
            a_s = jnp.where(keep, pltpu.roll(a, TILE - sh, 0), 1.0)
            b_s = jnp.where(keep, pltpu.roll(b, TILE - sh, 0), 0.0)
        else:
            keep = row >= sh
            a_s = jnp.where(keep, pltpu.roll(a, sh, 0), 1.0)
            b_s = jnp.where(keep, pltpu.roll(b, sh, 0), 0.0)
        b = a * b_s + b
        a = a * a_s
        sh *= 2
    h = b + a * carry_scr[...]
    carry_scr[...] = h[0:1, :] if rev else h[TILE - 1:TILE, :]
    if rev:
        o_ref[...] = (h + hf_ref[...]) * jax.nn.gelu(gate_ref[...])
    else:
        o_ref[...] = h


def _blockdiag(w):
    h, a, b = w.shape
    return jnp.einsum('hab,hg->hagb', w, jnp.eye(h, dtype=w.dtype)).reshape(h * a, h * b)


def _lru_mixer(proj, conv_w, conv_b, lam, w_r, b_r, w_i, b_i):
    cx = COL_LRU // GROUP_W
    rows8 = TILE // SUBLANES
    last8 = N_TOK // SUBLANES - 1

    def const(shape):
        return pl.BlockSpec(shape, lambda s: (0,) * len(shape))

    def specs(f, d):
        return dict(
            pv=pl.BlockSpec((SUBLANES, GROUP_W), lambda s: (jnp.maximum(f(s) * rows8 - 1, 0), cx)),
            x=pl.BlockSpec((TILE, GROUP_W), lambda s: (f(s), cx)),
            nx=pl.BlockSpec((SUBLANES, GROUP_W), lambda s: (jnp.minimum((f(s) + 1) * rows8, last8), cx)),
            gate=pl.BlockSpec((TILE, GROUP_W), lambda s: (f(s), cx + 1)),
            out=pl.BlockSpec((TILE, GROUP_W), lambda s: (f(s), 0)),
        )

    def dir_params(d):
        return (conv_w, conv_b.reshape(1, GROUP_W),
                jax.nn.softplus(-lam[d].astype(F32)).reshape(1, GROUP_W),
                _blockdiag(w_r[d]), b_r[d].reshape(1, GROUP_W),
                _blockdiag(w_i[d]), b_i[d].reshape(1, GROUP_W))

    par_specs = [const((LRU_CONV, GROUP_W)), const((1, GROUP_W)), const((1, GROUP_W)),
                 const((GROUP_W, GROUP_W)), const((1, GROUP_W)),
                 const((GROUP_W, GROUP_W)), const((1, GROUP_W))]
    out_shape = jax.ShapeDtypeStruct((N_TOK, GROUP_W), F32)
    scratch = [pltpu.VMEM((1, GROUP_W), F32)]

    sp = specs(lambda s: s, 0)
    hf = pl.pallas_call(
        functools.partial(_lru_kernel, False),
        out_shape=out_shape, grid=(N_TILES,),
        in_specs=[sp['pv'], sp['x'], sp['nx']] + par_specs,
        out_specs=sp['out'], scratch_shapes=scratch, compiler_params=_params(), name="lru_fwd",
    )(proj, proj, proj, *dir_params(0))

    sp = specs(lambda s: _bwd_tile(s, 1, N_TILES), 1)
    return pl.pallas_call(
        functools.partial(_lru_kernel, True),
        out_shape=out_shape, grid=(N_TILES,),
        in_specs=[sp['pv'], sp['x'], sp['nx'], sp['gate'], sp['out']] + par_specs,
        out_specs=sp['out'], scratch_shapes=scratch, compiler_params=_params(), name="lru_bwd",
    )(proj, proj, proj, proj, hf, *dir_params(1))


RW_IN = 2048
RW_HALO = 64


def _head_ones():
    h = jnp.arange(GROUP_W) // RWKV_HEAD
    return (h[:, None] == h[None, :]).astype(BF16)


def _rwkv_prep_kernel(pv_ref, cur_ref, nx_ref, mu_ref, kk_ref, ka_ref, rk_ref, w0_ref, wup_ref,
                      a0_ref, aup_ref, gup_ref, ones_ref,
                      r_o, v_o, kkn_o, bonus_o, g_o, w_o, kd_o, b_o, ext_scr, z_scr):
    i = pl.program_id(0)
    is_ctx = i == 0
    ext_scr[0:RW_HALO, :] = pv_ref[...]
    ext_scr[RW_HALO:RW_HALO + TILE, :] = cur_ref[...]
    ext_scr[RW_HALO + TILE:, :] = nx_ref[...]

    row = lax.broadcasted_iota(jnp.int32, (TILE, LANES), 0)
    lane = lax.broadcasted_iota(jnp.int32, (TILE, LANES), 1)
    c4 = lane % 4
    one = jnp.ones((TILE, LANES), F32)
    zero = jnp.zeros((TILE, LANES), F32)

    def mask(c):
        return jnp.where(c, one, zero)

    up_rows = jnp.where(i == 1, mask(row >= RW_HALO), one)
    dn_rows = jnp.where(i == N_TILES - 1, mask(row < TILE - RW_HALO), one)
    m_up = jnp.where(is_ctx, zero, up_rows * mask(c4 == 0))
    m_dn = jnp.where(is_ctx, zero, dn_rows * mask(c4 == 1))
    m_lt = jnp.where(is_ctx, mask(row >= 1) * mask(c4 % 2 == 0),
                     mask(row % GRID_W != 0) * mask(c4 == 2))
    m_rt = jnp.where(is_ctx, mask(row <= TILE - 2) * mask(c4 % 2 == 1),
                     mask(row % GRID_W != GRID_W - 1) * mask(c4 == 3))

    for cb in range(RW_IN // LANES):
        sl = slice(cb * LANES, (cb + 1) * LANES)
        p = ext_scr[RW_HALO:RW_HALO + TILE, sl]
        shifted = (ext_scr[0:TILE, sl] * m_up
                   + ext_scr[2 * RW_HALO:2 * RW_HALO + TILE, sl] * m_dn
                   + ext_scr[RW_HALO - 1:RW_HALO - 1 + TILE, sl] * m_lt
                   + ext_scr[RW_HALO + 1:RW_HALO + 1 + TILE, sl] * m_rt)
        z_scr[:, sl] = p + (shifted - p) * mu_ref[:, sl]

    gw = GROUP_W
    r = z_scr[:, 0:gw]
    k = z_scr[:, gw:2 * gw]
    v = z_scr[:, 2 * gw:3 * gw]
    wc = z_scr[:, 3 * gw:3 * gw + LANES]
    ac = z_scr[:, 3 * gw + LANES:3 * gw + 2 * LANES]
    gc = z_scr[:, 3 * gw + 2 * LANES:]
    ones = ones_ref[...]

    r_o[...] = r
    v_o[...] = v
    g_o[...] = _dot(jax.nn.sigmoid(gc).astype(BF16), gup_ref[...])
    kk = k * kk_ref[...]
    ss = _dot_sel(kk * kk, ones)
    kk = kk * lax.rsqrt(jnp.maximum(ss, 1e-12))
    kkn_o[...] = kk
    bonus_o[...] = _dot_sel(r * k * rk_ref[...], ones) * v
    tw = jnp.tanh(wc)
    for d in range(2):
        w_log = -jax.nn.softplus(-(w0_ref[d] + _dot_x3(tw, wup_ref[d]))) - 0.5
        w_o[d] = -jnp.exp(w_log)
        a = jax.nn.sigmoid(a0_ref[d] + _dot_x3(ac, aup_ref[d]))
        kd_o[d] = k * (1.0 + (a - 1.0) * ka_ref[...])
        b_o[d] = kk * a


def _rwkv_prep(proj, mu, k_k, k_a, r_k, w0, w_up, a0, a_up, g_up):
    blk64 = TILE // RW_HALO
    last64 = N_TOK // RW_HALO - 1

    def const(shape):
        return pl.BlockSpec(shape, lambda i: (0,) * len(shape))

    tok = pl.BlockSpec((TILE, GROUP_W), lambda i: (i, 0))
    tok2 = pl.BlockSpec((2, TILE, GROUP_W), lambda i: (0, i, 0))
    one = jax.ShapeDtypeStruct((N_TOK, GROUP_W), F32)
    two = jax.ShapeDtypeStruct((2, N_TOK, GROUP_W), F32)
    pad = LANES - RWKV_RANK

    def pad_rows(w):
        return jnp.pad(w.astype(F32), ((0, 0), (0, pad), (0, 0)))

    def pad_mu(m):
        z = jnp.zeros((pad,), F32)
        g3 = 3 * GROUP_W
        return jnp.concatenate([m[:g3], m[g3:g3 + RWKV_RANK], z,
                                m[g3 + RWKV_RANK:g3 + 2 * RWKV_RANK], z,
                                m[g3 + 2 * RWKV_RANK:]]).reshape(1, RW_IN)

    return pl.pallas_call(
        _rwkv_prep_kernel,
        out_shape=(one, one, one, one, one, two, two, two),
        grid=(N_TILES,),
        in_specs=[
            pl.BlockSpec((RW_HALO, RW_IN), lambda i: (jnp.maximum(i * blk64 - 1, 0), 0)),
            pl.BlockSpec((TILE, RW_IN), lambda i: (i, 0)),
            pl.BlockSpec((RW_HALO, RW_IN), lambda i: (jnp.minimum((i + 1) * blk64, last64), 0)),
            const((1, RW_IN)), const((1, GROUP_W)), const((1, GROUP_W)), const((1, GROUP_W)),
            const((2, 1, GROUP_W)), const((2, LANES, GROUP_W)),
            const((2, 1, GROUP_W)), const((2, LANES, GROUP_W)),
            const((RWKV_GATE_RANK, GROUP_W)), const((GROUP_W, GROUP_W)),
        ],
        out_specs=(tok, tok, tok, tok, tok, tok2, tok2, tok2),
        scratch_shapes=[pltpu.VMEM((TILE + 2 * RW_HALO, RW_IN), F32),
                        pltpu.VMEM((TILE, RW_IN), F32)],
        compiler_params=_params(), name="rwkv_prep",
    )(proj, proj, proj, pad_mu(mu.astype(F32)), k_k.reshape(1, GROUP_W), k_a.reshape(1, GROUP_W),
      r_k.reshape(1, GROUP_W), w0.reshape(2, 1, GROUP_W), pad_rows(w_up),
      a0.reshape(2, 1, GROUP_W), pad_rows(a_up), g_up.astype(BF16), _head_ones())


RW_C = 64
RW_T2 = 256
RW_NP = 2


def _split3(x):
    h1 = x.astype(BF16)
    h2 = (x - h1.astype(F32)).astype(BF16)
    h3 = (x - h1.astype(F32) - h2.astype(F32)).astype(BF16)
    return h1, h2, h3


def _rwkv_group_kernel(rf, vf, kf, lwf, kdf, bf, rb, vb, kb, lwb, kdb, bb, yf_o, yb_o, st_scr):
    @pl.when(pl.program_id(1) == 0)
    def _():
        st_scr[...] = jnp.zeros_like(st_scr)

    c = RW_C
    n_ch = RW_T2 // c
    g4 = 4 * c
    lane = lax.broadcasted_iota(jnp.int32, (1, LANES), 1)
    m0 = jnp.where(lane < RWKV_HEAD, 1.0, 0.0)
    m1 = jnp.where(lane >= RWKV_HEAD, 1.0, 0.0)
    bi = lax.broadcasted_iota(jnp.int32, (LANES, LANES), 0)
    bj = lax.broadcasted_iota(jnp.int32, (LANES, LANES), 1)
    bdmask = jnp.where(bi // RWKV_HEAD == bj // RWKV_HEAD, 1.0, 0.0)
    eye_k = jnp.where(bi == bj, 1.0, 0.0)
    tri_i = jnp.where(bi < c, jnp.where(bj <= bi, 1, 0), jnp.where(bj >= bi, 1, 0))
    tri_g = jnp.where((bi // c == bj // c) & (tri_i == 1), 1.0, 0.0).astype(BF16)
    is_fwd_row = lax.broadcasted_iota(jnp.int32, (2 * c, LANES), 0) < c

    ri = lax.broadcasted_iota(jnp.int32, (g4, g4), 0)
    ci_ = lax.broadcasted_iota(jnp.int32, (g4, g4), 1)
    same = ri // c == ci_ // c
    t_in, s_in = ri % c, ci_ % c
    fwd_blk = ri < 2 * c
    strict_i = jnp.where(fwd_blk, jnp.where(s_in < t_in, 1, 0), jnp.where(s_in > t_in, 1, 0))
    strict4 = same & (strict_i == 1)
    incl_f = same & fwd_blk & (s_in == t_in)
    ymask4 = strict4 | incl_f
    eye4 = jnp.where(ri == ci_, 1.0, 0.0)

    def nt(a, b):
        return lax.dot_general(a, b, (((1,), (1,)), ((), ())), preferred_element_type=F32)

    def mm1(a, b):
        return _dot(a.astype(BF16), b.astype(BF16))

    zero = jnp.zeros((c, LANES), F32)

    def x4(z):
        za, zb = z[:c], z[c:]
        return jnp.concatenate([
            jnp.concatenate([za * m0, zero], axis=1), jnp.concatenate([za * m1, zero], axis=1),
            jnp.concatenate([zero, zb * m0], axis=1), jnp.concatenate([zero, zb * m1], axis=1)], axis=0)

    def head_sum(x, lo):
        a = x[0:c] + x[c:2 * c]
        b = x[2 * c:3 * c] + x[3 * c:]
        return jnp.concatenate([
            jnp.concatenate([a[:, lo:lo + LANES], a[:, 2 * LANES + lo:3 * LANES + lo]], axis=1),
            jnp.concatenate([b[:, LANES + lo:2 * LANES + lo],
                             b[:, 3 * LANES + lo:4 * LANES + lo]], axis=1)], axis=0)

    refs_f = (rf, vf, kf, lwf, kdf, bf)
    refs_b = (rb, vb, kb, lwb, kdb, bb)
    groups = [(pr, ci) for ci in range(n_ch) for pr in range(RW_NP)]
    pre = {}
    for pr, ci in groups:
        cols = slice(pr * LANES, (pr + 1) * LANES)
        ra = slice(ci * c, (ci + 1) * c)
        rb_ = slice((n_ch - 1 - ci) * c, (n_ch - ci) * c)

        def both(idx):
            return jnp.concatenate([refs_f[idx][ra, cols], refs_b[idx][rb_, cols]], axis=0)

        r, v, kk, lw, kd, b = (both(i) for i in range(6))
        l1, l2, l3 = _split3(lw)
        logpi = _dot(tri_g, l1) + _dot(tri_g, l2) + _dot(tri_g, l3)
        pe = jnp.exp(logpi - lw)
        inv = jnp.exp(-logpi)
        qs, ks, bs = x4(kk * pe), x4(kd * inv), x4(b * inv)
        rs = x4(r * jnp.where(is_fwd_row, jnp.exp(logpi), pe))
        vs = x4(v)
        qb, kb_, bb_, rb2 = qs.astype(BF16), ks.astype(BF16), bs.astype(BF16), rs.astype(BF16)
        pre[(pr, ci)] = dict(
            qs=qs, rs=rs, vs=vs, v=v, kt=kd * inv, bt=b * inv,
            ak=jnp.where(strict4, nt(qb, kb_), 0.0), ab=jnp.where(strict4, nt(qb, bb_), 0.0),
            mk=jnp.where(ymask4, nt(rb2, kb_), 0.0), mb=jnp.where(ymask4, nt(rb2, bb_), 0.0),
            ptot=[jnp.exp(jnp.sum(lw[:c].T, axis=1, keepdims=True)),
                  jnp.exp(jnp.sum(lw[c:].T, axis=1, keepdims=True))])

    for key in groups:
        pre[key]['pw'] = pre[key]['ab']
        pre[key]['x'] = eye4 - pre[key]['ab']
    for _ in range(5):
        for key in groups:
            pre[key]['pw'] = mm1(pre[key]['pw'], pre[key]['pw'])
        for key in groups:
            pre[key]['x'] = pre[key]['x'] + mm1(pre[key]['x'], pre[key]['pw'])

    for key in groups:
        pc = pre[key]
        rhs = jnp.concatenate([mm1(pc['ak'], pc['vs']), pc['qs']], axis=1)
        t_rhs = mm1(pc['x'], rhs)
        yh = jnp.concatenate([mm1(pc['mk'], pc['vs']), pc['rs']], axis=1) - mm1(pc['mb'], t_rhs)
        wtq = head_sum(t_rhs, 0)
        pc['ypre'] = head_sum(yh, 0)
        kts = jnp.concatenate([jnp.concatenate([pc['kt'][:c], zero], axis=1),
                               jnp.concatenate([zero, pc['kt'][c:]], axis=1)], axis=0)
        bts = jnp.concatenate([jnp.concatenate([pc['bt'][:c], zero], axis=1),
                               jnp.concatenate([zero, pc['bt'][c:]], axis=1)], axis=0)
        ktv = mm1(kts.T, pc['v'])
        bw = mm1(bts.T, wtq)
        pc['cst'] = [bdmask * (ktv[u * LANES:(u + 1) * LANES] - bw[u * LANES:(u + 1) * LANES, :LANES])
                     for u in range(2)]
        pc['mtx'] = [eye_k - bdmask * bw[u * LANES:(u + 1) * LANES, LANES:] for u in range(2)]

    state = {(d, pr): st_scr[d, pr] for d in range(2) for pr in range(RW_NP)}
    for pr, ci in groups:
        pc = pre[(pr, ci)]
        for d in range(2):
            s0 = state[(d, pr)]
            yp = pc['ypre'][d * c:(d + 1) * c]
            y = yp[:, :LANES] + mm1(yp[:, LANES:], s0)
            lo = (ci if d == 0 else n_ch - 1 - ci) * c
            (yf_o if d == 0 else yb_o)[lo:lo + c, pr * LANES:(pr + 1) * LANES] = y
            state[(d, pr)] = pc['ptot'][d] * (_dot_x3(pc['mtx'][d], s0) + pc['cst'][d])
    for (d, pr), val in state.items():
        st_scr[d, pr] = val


def _rwkv_chunks(r, v, kk, lw, kd, b):
    n_t = N_TOK // RW_T2
    ft = lambda s: s
    bt = lambda s: _bwd_tile(s, CTX_LEN // RW_T2, n_t)

    bw = RW_NP * LANES

    def one(f):
        return pl.BlockSpec((RW_T2, bw), lambda p, s: (f(s), p))

    def two(f, d):
        return pl.BlockSpec((None, RW_T2, bw), lambda p, s: (d, f(s), p))

    out = jax.ShapeDtypeStruct((N_TOK, GROUP_W), F32)
    return pl.pallas_call(
        _rwkv_group_kernel,
        out_shape=(out, out),
        grid=(GROUP_W // bw, n_t),
        in_specs=[one(ft), one(ft), one(ft), two(ft, 0), two(ft, 0), two(ft, 0),
                  one(bt), one(bt), one(bt), two(bt, 1), two(bt, 1), two(bt, 1)],
        out_specs=(one(ft), one(bt)),
        scratch_shapes=[pltpu.VMEM((2, RW_NP, LANES, LANES), F32)],
        compiler_params=_params(2), name="rwkv_chunks",
    )(r, v, kk, lw, kd, b, r, v, kk, lw, kd, b)


def _merge_kernel(ya_ref, yb_ref, yc_ref, ydf_ref, ydb_ref, bonus_ref, g_ref, lnw_ref, lnb_ref,
                  ones_ref, gain_ref, wout_ref, xs_ref, mod_ref, n2_ref, rw_ref, rb_ref,
                  xo_ref, fx_ref, te_ref, tg_ref):
    i = pl.program_id(0)
    is_ctx = _row_is_ctx(i, TILE)
    ones = ones_ref[...]
    inv = 1.0 / RWKV_HEAD
    yd = ydf_ref[...] + ydb_ref[...]
    mean = _dot_sel(yd, ones) * inv
    dl = yd - mean
    var = _dot_sel(dl * dl, ones) * inv
    yd = (dl * lax.rsqrt(var + RWKV_LN_EPS) * lnw_ref[...] + lnb_ref[...] + bonus_ref[...]) * g_ref[...]

    parts = []
    for gi, y in enumerate((ya_ref[...], yb_ref[...], yc_ref[...], yd)):
        parts.append((_rms_rows(y) * gain_ref[:, gi * GROUP_W:(gi + 1) * GROUP_W]).astype(BF16))
    m = _dot(jnp.concatenate(parts, axis=1), wout_ref[...])
    xs = xs_ref[...] + _mod_rows(mod_ref, 2, is_ctx) * m
    xo_ref[...] = xs

    fx = _rms_rows(xs) * n2_ref[...]
    fx = fx * (1.0 + _mod_rows(mod_ref, 4, is_ctx)) + _mod_rows(mod_ref, 3, is_ctx)
    fx_ref[...] = fx

    logits = _dot_x3(fx, rw_ref[...]) + rb_ref[...]
    lane = lax.broadcasted_iota(jnp.int32, logits.shape, 1)
    vals = logits
    tops, idxs = [], []
    for _ in range(TOP_K):
        mx = jnp.max(vals, axis=-1, keepdims=True)
        ix = jnp.min(jnp.where(vals == mx, lane, LANES), axis=-1, keepdims=True)
        tops.append(mx)
        idxs.append(ix)
        vals = jnp.where(lane == ix, -jnp.inf, vals)
    es = [jnp.exp(t - tops[0]) for t in tops]
    den = es[0] + es[1] + es[2] + es[3]
    te = jnp.zeros(logits.shape, jnp.int32)
    tg = jnp.zeros(logits.shape, F32)
    for kx in range(TOP_K):
        te = jnp.where(lane == kx, idxs[kx], te)
        tg = jnp.where(lane == kx, es[kx] / den, tg)
    te_ref[...] = te
    tg_ref[...] = tg


def _merge(ya, yb, yc, ydf, ydb, bonus, g, ln_w, ln_b, gain, w_out_b, xs, mod_l, norm2,
           router_w, router_b):
    def const(shape):
        return pl.BlockSpec(shape, lambda i: (0,) * len(shape))

    grp = pl.BlockSpec((TILE, GROUP_W), lambda i: (i, 0))
    full = pl.BlockSpec((TILE, D_MODEL), lambda i: (i, 0))
    lanes = pl.BlockSpec((TILE, LANES), lambda i: (i, 0))
    rw = jnp.pad(router_w.astype(F32), ((0, 0), (0, LANES - N_EXPERTS)))
    rb = jnp.concatenate([router_b.astype(F32), jnp.full((LANES - N_EXPERTS,), -1e30, F32)])
    return pl.pallas_call(
        _merge_kernel,
        out_shape=(jax.ShapeDtypeStruct((N_TOK, D_MODEL), F32),
                   jax.ShapeDtypeStruct((N_TOK, D_MODEL), F32),
                   jax.ShapeDtypeStruct((N_TOK, LANES), jnp.int32),
                   jax.ShapeDtypeStruct((N_TOK, LANES), F32)),
        grid=(N_TILES,),
        in_specs=[grp] * 7 + [const((1, GROUP_W)), const((1, GROUP_W)), const((GROUP_W, GROUP_W)),
                              const((1, D_MODEL)), const((D_MODEL, D_MODEL)), full,
                              const((SUBLANES, 6 * D_MODEL)), const((1, D_MODEL)),
                              const((D_MODEL, LANES)), const((1, LANES))],
        out_specs=(full, full, lanes, lanes),
        compiler_params=_params(), name="merge",
    )(ya, yb, yc, ydf, ydb, bonus, g, ln_w.reshape(1, GROUP_W), ln_b.reshape(1, GROUP_W),
      _head_ones(), gain.reshape(1, D_MODEL), w_out_b, xs, mod_l, norm2.reshape(1, D_MODEL),
      rw, rb.reshape(1, LANES))


N_SLOTS = N_TOK * TOP_K
MOE_NBLK = N_SLOTS // MOE_BLOCK + N_EXPERTS
MOE_ROWS = MOE_NBLK * MOE_BLOCK
MOE_OUT_ROWS = N_SLOTS + 2 * MOE_BLOCK


ROUTE_PARTS = 2


def _route_kernel(part, e_ref, cur0_ref, tok0_hbm, dst0_hbm, tok_ref, dst_ref, cur_ref, sem):
    copies = [pltpu.make_async_copy(tok0_hbm, tok_ref, sem.at[0]),
              pltpu.make_async_copy(dst0_hbm, dst_ref, sem.at[1])]
    for cp in copies:
        cp.start()
    for cp in copies:
        cp.wait()
    for e in range(N_EXPERTS):
        cur_ref[e] = cur0_ref[e]

    def place(t, c):
        es = [e_ref[t * TOP_K + k] for k in range(TOP_K)]
        ps = [cur_ref[e] for e in es]
        for k in range(TOP_K):
            cur_ref[es[k]] = ps[k] + 1
        for k in range(TOP_K):
            tok_ref[ps[k]] = t
            dst_ref[ps[k] + MOE_BLOCK] = k * N_TOK + t
        return c
    n = N_TOK // ROUTE_PARTS
    lax.fori_loop(part * n, (part + 1) * n, place, 0)


def _route(top_e):
    assert MOE_BLOCK == 256 and TOP_K == 4
    flat_e = top_e.reshape(N_SLOTS)
    counts = jnp.sum((flat_e[:, None] == jnp.arange(N_EXPERTS, dtype=jnp.int32)[None, :])
                     .astype(jnp.int32), axis=0)
    padded = (counts + MOE_BLOCK - 1) // MOE_BLOCK * MOE_BLOCK
    ex = jnp.arange(N_EXPERTS)
    pends = jnp.sum(jnp.where(ex[None, :] <= ex[:, None], padded[None, :], 0), axis=1)
    smem = pl.BlockSpec(memory_space=pltpu.SMEM)
    row = jnp.arange(MOE_ROWS + MOE_BLOCK, dtype=jnp.int32) - MOE_BLOCK
    dump = N_SLOTS + ((row // MOE_BLOCK) % 2) * MOE_BLOCK + row % MOE_BLOCK
    cur = (pends - padded).astype(jnp.int32)
    rows_tok, rows_dst = jnp.zeros((MOE_ROWS,), jnp.int32), dump
    for part in range(ROUTE_PARTS):
        rows_tok, rows_dst, cur = pl.pallas_call(
            functools.partial(_route_kernel, part),
            out_shape=(jax.ShapeDtypeStruct((MOE_ROWS,), jnp.int32),
                       jax.ShapeDtypeStruct((MOE_ROWS + MOE_BLOCK,), jnp.int32),
                       jax.ShapeDtypeStruct((N_EXPERTS,), jnp.int32)),
            in_specs=[smem, smem, pl.BlockSpec(memory_space=pl.ANY), pl.BlockSpec(memory_space=pl.ANY)],
            out_specs=(smem, smem, smem),
            scratch_shapes=[pltpu.SemaphoreType.DMA((2,))],
            name="route",
        )(flat_e, cur, rows_tok, rows_dst)
    blk_row0 = jnp.arange(MOE_NBLK, dtype=jnp.int32) * MOE_BLOCK
    block_e = jnp.minimum(jnp.sum((pends[None, :] <= blk_row0[:, None]).astype(jnp.int32), axis=1),
                          N_EXPERTS - 1)
    n_used = (pends[-1] // MOE_BLOCK).astype(jnp.int32).reshape(1)
    return block_e, n_used, rows_tok, rows_dst


def _moe_kernel(be_ref, nu_ref, tok_ref, dst_ref, fx_hbm, w1_ref, b1_ref, w2_ref,
                b2_ref, y_hbm, xg_scr, yb_scr, gsem, ssem):
    i = pl.program_id(0)
    n_used = nu_ref[0]
    slot = i % 2

    def gather_start(blk, sl, r):
        tok = tok_ref[blk * MOE_BLOCK + r]
        pltpu.make_async_copy(fx_hbm.at[pl.ds(tok, 1), :],
                              xg_scr.at[sl, pl.ds(r, 1), :], gsem.at[sl]).start()

    def scatter_start(blk, sl, r):
        dst = dst_ref[(blk + 1) * MOE_BLOCK + r]
        pltpu.make_async_copy(yb_scr.at[sl, pl.ds(r, 1), :],
                              y_hbm.at[pl.ds(dst, 1), :], ssem.at[sl]).start()

    def block_copy(sl, sem):
        return pltpu.make_async_copy(yb_scr.at[sl], y_hbm.at[pl.ds(N_SLOTS, MOE_BLOCK), :],
                                     sem.at[sl])

    def for_rows(fn):
        def body(r, c):
            fn(r)
            return c
        lax.fori_loop(0, MOE_BLOCK, body, 0)

    @pl.when(i == 0)
    def _():
        yb_scr[...] = jnp.zeros_like(yb_scr)
        block_copy(0, ssem).start()
        for_rows(lambda r: gather_start(0, 0, r))

    @pl.when(i < n_used)
    def _():
        block_copy(slot, gsem).wait()
        x = xg_scr[slot].astype(BF16)

        n_batch = 8
        per = MOE_BLOCK // (n_batch // 2)

        def issue(bi):
            fn, blk = (gather_start, i + 1) if bi % 2 == 0 else (scatter_start, i - 1)
            for r in range((bi // 2) * per, (bi // 2 + 1) * per):
                fn(blk, 1 - slot, r)

        ff_cuts = ((0, 512), (512, D_FF))
        hid = []
        for half in range(2):
            for ci, (a, b) in enumerate(ff_cuts):
                cols = slice(half * D_FF + a, half * D_FF + b)
                hid.append(_dot(x, w1_ref[:, cols]) + b1_ref[:, cols])
                issue(half * 2 + ci)
        x_glu = jnp.minimum(jnp.concatenate(hid[:2], axis=1), SWIGLU_LIMIT)
        x_lin = jnp.clip(jnp.concatenate(hid[2:], axis=1), -SWIGLU_LIMIT, SWIGLU_LIMIT)
        act = (x_glu * jax.nn.sigmoid(SWIGLU_ALPHA * x_glu) * (x_lin + 1.0)).astype(BF16)
        ys = []
        n_out = 4
        wo = D_MODEL // n_out
        for n in range(n_out):
            cols = slice(n * wo, (n + 1) * wo)
            ys.append(_dot(act, w2_ref[:, cols].astype(BF16)) + b2_ref[:, cols])
            issue(4 + n)
        block_copy(slot, ssem).wait()
        yb_scr[slot] = jnp.concatenate(ys, axis=1)

    @pl.when(i == n_used)
    def _():
        block_copy(slot, gsem).wait()
        block_copy(slot, ssem).wait()
        for_rows(lambda r: scatter_start(i - 1, 1 - slot, r))
        block_copy(1 - slot, ssem).wait()


def _moe(fx, route, l, w1_b, b1, w2_b, b2):
    block_e, n_used, rows_tok, rows_dst = route
    grid_spec = pltpu.PrefetchScalarGridSpec(
        num_scalar_prefetch=4,
        grid=(MOE_NBLK,),
        in_specs=[
            pl.BlockSpec(memory_space=pl.ANY),
            pl.BlockSpec((None, None, D_MODEL, 2 * D_FF), lambda i, be, *_: (l, be[i], 0, 0)),
            pl.BlockSpec((None, None, 1, 2 * D_FF), lambda i, be, *_: (l, be[i], 0, 0)),
            pl.BlockSpec((None, None, D_FF, D_MODEL), lambda i, be, *_: (l, be[i], 0, 0)),
            pl.BlockSpec((None, None, 1, D_MODEL), lambda i, be, *_: (l, be[i], 0, 0)),
        ],
        out_specs=pl.BlockSpec(memory_space=pl.ANY),
        scratch_shapes=[pltpu.VMEM((2, MOE_BLOCK, D_MODEL), F32),
                        pltpu.VMEM((2, MOE_BLOCK, D_MODEL), F32),
                        pltpu.SemaphoreType.DMA((2,)),
                        pltpu.SemaphoreType.DMA((2,))],
    )
    return pl.pallas_call(
        _moe_kernel,
        out_shape=jax.ShapeDtypeStruct((MOE_OUT_ROWS, D_MODEL), F32),
        grid_spec=grid_spec,
        compiler_params=_params(), name="moe",
    )(block_e, n_used, rows_tok, rows_dst, fx, w1_b,
      b1.reshape(DEPTH, N_EXPERTS, 1, 2 * D_FF), w2_b, b2.reshape(DEPTH, N_EXPERTS, 1, D_MODEL))


def _combine_kernel(final, tile0, y0_ref, y1_ref, y2_ref, y3_ref, tg_ref, xs_ref, mod_ref, fn_ref,
                    o_ref):
    is_ctx = _row_is_ctx(pl.program_id(0) + tile0, TILE)
    f = None
    for k, y_ref in enumerate((y0_ref, y1_ref, y2_ref, y3_ref)):
        yk = y_ref[...] * tg_ref[:, k:k + 1]
        f = yk if f is None else f + yk
    xs = xs_ref[...] + _mod_rows(mod_ref, 5, is_ctx) * f
    if final:
        xs = _rms_rows(xs) * fn_ref[...]
    o_ref[...] = xs


def _combine(y4, top_g, xs, mod_l, final_norm, final):
    tile0 = CTX_LEN // TILE if final else 0
    n_out = SEQ if final else N_TOK
    return pl.pallas_call(
        functools.partial(_combine_kernel, final, tile0),
        out_shape=jax.ShapeDtypeStruct((n_out, D_MODEL), F32),
        grid=(n_out // TILE,),
        in_specs=[pl.BlockSpec((TILE, D_MODEL), functools.partial(lambda k, i: (k * N_TILES + i + tile0, 0), k))
                  for k in range(TOP_K)] + [
                  pl.BlockSpec((TILE, LANES), lambda i: (i + tile0, 0)),
                  pl.BlockSpec((TILE, D_MODEL), lambda i: (i + tile0, 0)),
                  pl.BlockSpec((SUBLANES, 6 * D_MODEL), lambda i: (0, 0)),
                  pl.BlockSpec((1, D_MODEL), lambda i: (0, 0))],
        out_specs=pl.BlockSpec((TILE, D_MODEL), lambda i: (i, 0)),
        compiler_params=_params(), name="combine_final" if final else "combine",
    )(y4, y4, y4, y4, top_g, xs, mod_l, final_norm.reshape(1, D_MODEL))


def _w_in_layout(w):
    g = GROUP_W
    s5, ret, lru, rw = w[:, :g], w[:, g:5 * g], w[:, 5 * g:7 * g], w[:, 7 * g:]
    z = jnp.zeros((D_MODEL, LANES - RWKV_RANK), w.dtype)
    rw = jnp.concatenate([rw[:, :3 * g], rw[:, 3 * g:3 * g + RWKV_RANK], z,
                          rw[:, 3 * g + RWKV_RANK:3 * g + 2 * RWKV_RANK], z,
                          rw[:, 3 * g + 2 * RWKV_RANK:]], axis=1)
    return jnp.concatenate([rw, s5, ret, lru], axis=1).astype(BF16)


def kernel(x, c, ctx, c_ctx, w_ada, b_ada, norm1, norm2, w_in, w_out, mix_gain, s5_lam_re, s5_lam_im, s5_log_dt, s5_b_re, s5_b_im, s5_c_re, s5_c_im, s5_d, s5_w_glu, s5_b_glu, ret_decay, lru_conv_w, lru_conv_b, lru_lam, lru_w_r, lru_b_r, lru_w_i, lru_b_i, rwkv_mu, rwkv_w0, rwkv_w_up, rwkv_a0, rwkv_a_up, rwkv_g_up, rwkv_k_k, rwkv_k_a, rwkv_r_k, rwkv_ln_w, rwkv_ln_b, router_w, router_b, exp_w1, exp_b1, exp_w2, exp_b2, final_norm):
    assert x.shape == (1, SEQ, D_MODEL) and ctx.shape == (1, CTX_LEN, D_MODEL)
    xs = jnp.concatenate([ctx[0], x[0]], axis=0).astype(F32)
    cc = jnp.zeros((SUBLANES, D_MODEL), F32).at[0].set(c[0]).at[1].set(c_ctx)
    mods = _modulation(cc, w_ada, b_ada)
    rope = _rope_tables()
    w1_b = exp_w1.astype(BF16)

    for l in range(DEPTH):
        mod_l = mods[l]
        proj = _inproj(xs, norm1[l], mod_l, _w_in_layout(w_in[l]))

        s5_tabs = [_s5_tables(s5_lam_re[l, d], s5_lam_im[l, d], s5_log_dt[l, d], s5_b_re[l, d],
                              s5_b_im[l, d], s5_c_re[l, d], s5_c_im[l, d], d == 1) for d in range(2)]
        ya = _s5_mixer(proj, s5_tabs[0], s5_tabs[1], s5_d[l], s5_w_glu[l], s5_b_glu[l])

        ret_f, ret_b = _ret_tables(ret_decay[l])
        yb = _ret_mixer(proj, rope, ret_f, ret_b)

        yc = _lru_mixer(proj, lru_conv_w[l], lru_conv_b[l], lru_lam[l], lru_w_r[l], lru_b_r[l],
                        lru_w_i[l], lru_b_i[l])

        r, v, kk, bonus, g, w, kd, b = _rwkv_prep(
            proj, rwkv_mu[l], rwkv_k_k[l], rwkv_k_a[l], rwkv_r_k[l], rwkv_w0[l], rwkv_w_up[l],
            rwkv_a0[l], rwkv_a_up[l], rwkv_g_up[l])
        ydf, ydb = _rwkv_chunks(r, v, kk, w, kd, b)

        xs, fx, top_e, top_g = _merge(ya, yb, yc, ydf, ydb, bonus, g, rwkv_ln_w[l], rwkv_ln_b[l],
                                      mix_gain[l], w_out[l].astype(BF16), xs, mod_l, norm2[l],
                                      router_w[l], router_b[l])
        route = _route(top_e[:, :TOP_K])
        y4 = _moe(fx, route, l, w1_b, exp_b1, exp_w2, exp_b2)
        xs = _combine(y4, top_g, xs, mod_l, final_norm, l == DEPTH - 1)

    return xs.reshape(1, SEQ, D_MODEL)
```
